```python
import math
import jax, jax.numpy as jnp
from jax import lax
import numpy as np

D_MODEL = 1024
BATCH = 32
SEQ = 2048
DEPTH = 1
DEC_BATCH = 16
DEC_SEQ = 4096
PAST_LEN = 128

D_CONV = D_MODEL // 2
D_HYENA = D_MODEL // 2
CONV_KERNEL = 31
SHORT_KERNEL = 3
HYENA_ORDER = 2
N_BANDS = 16
FILTER_EMB = 1 + 2 * N_BANDS
FILTER_HIDDEN = 64
N_DIRS = 2
D_FF = 4 * D_MODEL
N_MOD = 6
LN_EPS = 1e-5
DECAY_TARGET = 1e-2
FAST_DECAY_PCT = 0.3
SLOW_DECAY_PCT = 1.5
MAX_DECAY = math.log(DECAY_TARGET) / FAST_DECAY_PCT
MIN_DECAY = math.log(DECAY_TARGET) / SLOW_DECAY_PCT
IN_WIDTH = 2 * D_CONV + (HYENA_ORDER + 1) * D_HYENA + 2 * D_MODEL
DEEPNORM_ALPHA = (2.0 * DEPTH) ** 0.25
DEEPNORM_BETA = (8.0 * DEPTH) ** -0.25

kernel_name = "conditioned_conformer_hyena_hybrid_encoder"


def _ln(x, g=None, b=None):
    xf = x.astype(jnp.float32)
    mu = jnp.mean(xf, axis=-1, keepdims=True)
    var = jnp.mean(jnp.square(xf - mu), axis=-1, keepdims=True)
    y = (xf - mu) * lax.rsqrt(var + LN_EPS)
    if g is not None:
        y = y * g.astype(jnp.float32) + b.astype(jnp.float32)
    return y.astype(x.dtype)


def _dwconv(x, w, b):
    k = w.shape[0]
    pad = k // 2
    y = lax.conv_general_dilated(x, w[:, None, :].astype(x.dtype), window_strides=(1,),
                                 padding=[(pad, pad)], dimension_numbers=('NWC', 'WIO', 'NWC'),
                                 feature_group_count=x.shape[-1])
    return y + b.astype(x.dtype)


def _implicit_filters(L, w1, b1, freq, w2, b2, w3):
    f32 = jnp.float32
    t = jnp.linspace(0.0, 1.0, L, dtype=f32)[:, None]
    w = (2.0 * math.pi / L) * jnp.arange(L, dtype=f32)[:, None]
    bands = jnp.linspace(1e-4, N_BANDS - 1, N_BANDS, dtype=f32)[None, :]
    z = jnp.concatenate([t, jnp.cos(w * bands), -jnp.sin(w * bands)], axis=-1)
    fr = freq.astype(f32)
    hdn = jnp.sin(fr * (z @ w1.astype(f32) + b1.astype(f32)))
    hdn = jnp.sin(fr * (hdn @ w2.astype(f32) + b2.astype(f32)))
    h = (hdn @ w3.astype(f32)).reshape(L, HYENA_ORDER, N_DIRS, D_HYENA)
    deltas = jnp.abs(jnp.linspace(MIN_DECAY, MAX_DECAY, D_HYENA, dtype=f32))
    decay_f = jnp.exp(-t * deltas)[:, None, :]
    decay_b = jnp.exp(-t * deltas[::-1])[:, None, :]
    h_f = h[:, :, 0] * decay_f
    h_b = h[:, :, 1] * decay_b
    k = jnp.concatenate([h_f, jnp.zeros((1, HYENA_ORDER, D_HYENA), f32), h_b[1:][::-1]], axis=0)
    return jnp.fft.rfft(k, axis=0)


def _fftconv(u, hf, skip):
    L = u.shape[1]
    uf32 = u.astype(jnp.float32)
    uf = jnp.fft.rfft(uf32, n=2 * L, axis=1)
    y = jnp.fft.irfft(uf * hf[None], n=2 * L, axis=1)[:, :L]
    return (y + uf32 * skip.astype(jnp.float32)).astype(u.dtype)


def _mixer(h, w_in, conv_dw_w, conv_dw_b, conv_ln_g, conv_ln_b, conv_pw_w,
           hy_short_w, hy_short_b, filt, hy_skip, hy_out_w, w_out):
    proj = jnp.einsum('bld,de->ble', h, w_in)
    s0 = D_CONV
    s1 = 2 * D_CONV
    s2 = s1 + (HYENA_ORDER + 1) * D_HYENA
    s3 = s2 + D_MODEL
    a_val, a_gate, hy_in, g_a, g_b = jnp.split(proj, [s0, s1, s2, s3], axis=-1)
    a = a_val * jax.nn.sigmoid(a_gate)
    a = _dwconv(a, conv_dw_w, conv_dw_b)
    a = jax.nn.silu(_ln(a, conv_ln_g, conv_ln_b))
    y_a = jnp.einsum('blc,cd->bld', a, conv_pw_w)
    u = _dwconv(hy_in, hy_short_w, hy_short_b)
    v, x1, x2 = jnp.split(u, HYENA_ORDER + 1, axis=-1)
    zb = x1 * _fftconv(v, filt[:, 0], hy_skip[0])
    zb = x2 * _fftconv(zb, filt[:, 1], hy_skip[1])
    y_b = jnp.einsum('blc,cd->bld', zb, hy_out_w)
    m = jax.nn.sigmoid(g_a) * y_a + jax.nn.sigmoid(g_b) * y_b
    return jnp.einsum('bld,de->ble', m, w_out)


def _encoder(x, c, params):
    (w_ada, b_ada, w_in, conv_dw_w, conv_dw_b, conv_ln_g, conv_ln_b, conv_pw_w,
     hy_short_w, hy_short_b, hy_ffn_w1, hy_ffn_b1, hy_sin_freq, hy_ffn_w2, hy_ffn_b2,
     hy_ffn_w3, hy_skip, hy_out_w, w_out, ln1_g, ln1_b,
     mlp_w1, mlp_b1, mlp_w2, mlp_b2, ln2_g, ln2_b) = params
    L = x.shape[1]
    for l in range(DEPTH):
        mod = jnp.einsum('bd,de->be', jax.nn.silu(c), w_ada[l]) + b_ada[l]
        shift1, scale1, gate1, shift2, scale2, gate2 = [m[:, None, :] for m in jnp.split(mod, N_MOD, axis=-1)]
        filt = _implicit_filters(L, hy_ffn_w1[l], hy_ffn_b1[l], hy_sin_freq[l],
                                 hy_ffn_w2[l], hy_ffn_b2[l], hy_ffn_w3[l])
        h = _ln(x) * (1.0 + scale1) + shift1
        mix = _mixer(h, w_in[l], conv_dw_w[l], conv_dw_b[l], conv_ln_g[l], conv_ln_b[l], conv_pw_w[l],
                     hy_short_w[l], hy_short_b[l], filt, hy_skip[l], hy_out_w[l], w_out[l])
        x = _ln(DEEPNORM_ALPHA * x + gate1 * mix, ln1_g[l], ln1_b[l])
        h = _ln(x) * (1.0 + scale2) + shift2
        f = jnp.square(jax.nn.relu(jnp.einsum('bld,df->blf', h, mlp_w1[l]) + mlp_b1[l]))
        f = jnp.einsum('blf,fd->bld', f, mlp_w2[l]) + mlp_b2[l]
        x = _ln(DEEPNORM_ALPHA * x + gate2 * f, ln2_g[l], ln2_b[l])
    return x


def setup_inputs(seed: int = 0) -> dict:
    key = jax.random.key(seed)
    ks = iter(jax.random.split(key, 40))
    f32 = jnp.float32

    def nrm(shape, scale):
        return jax.random.normal(next(ks), shape, f32) * scale

    def gain(shape):
        return 1.0 + nrm(shape, 0.01)

    D = DEPTH
    return {
        "x_prompt": nrm((BATCH, SEQ, D_MODEL), 1.0),
        "x_sample": nrm((DEC_BATCH, DEC_SEQ, D_MODEL), 1.0),
        "c_prompt": nrm((BATCH, D_MODEL), 1.0),
        "c_sample": nrm((DEC_BATCH, D_MODEL), 1.0),
        "w_ada": nrm((D, D_MODEL, N_MOD * D_MODEL), 0.5 * D_MODEL ** -0.5),
        "b_ada": nrm((D, N_MOD * D_MODEL), 0.01),
        "w_in": nrm((D, D_MODEL, IN_WIDTH), D_MODEL ** -0.5),
        "conv_dw_w": nrm((D, CONV_KERNEL, D_CONV), CONV_KERNEL ** -0.5),
        "conv_dw_b": nrm((D, D_CONV), 0.01),
        "conv_ln_g": gain((D, D_CONV)),
        "conv_ln_b": nrm((D, D_CONV), 0.01),
        "conv_pw_w": nrm((D, D_CONV, D_MODEL), D_CONV ** -0.5),
        "hy_short_w": nrm((D, SHORT_KERNEL, (HYENA_ORDER + 1) * D_HYENA), SHORT_KERNEL ** -0.5),
        "hy_short_b": nrm((D, (HYENA_ORDER + 1) * D_HYENA), 0.01),
        "hy_ffn_w1": nrm((D, FILTER_EMB, FILTER_HIDDEN), FILTER_EMB ** -0.5),
        "hy_ffn_b1": nrm((D, FILTER_HIDDEN), 0.01),
        "hy_sin_freq": gain((D, FILTER_HIDDEN)),
        "hy_ffn_w2": nrm((D, FILTER_HIDDEN, FILTER_HIDDEN), FILTER_HIDDEN ** -0.5),
        "hy_ffn_b2": nrm((D, FILTER_HIDDEN), 0.01),
        "hy_ffn_w3": nrm((D, FILTER_HIDDEN, HYENA_ORDER * N_DIRS * D_HYENA), 0.05 * FILTER_HIDDEN ** -0.5),
        "hy_skip": nrm((D, HYENA_ORDER, D_HYENA), 0.5),
        "hy_out_w": nrm((D, D_HYENA, D_MODEL), D_HYENA ** -0.5),
        "w_out": nrm((D, D_MODEL, D_MODEL), DEEPNORM_BETA * D_MODEL ** -0.5),
        "ln1_g": gain((D, D_MODEL)),
        "ln1_b": nrm((D, D_MODEL), 0.01),
        "mlp_w1": nrm((D, D_MODEL, D_FF), D_MODEL ** -0.5),
        "mlp_b1": nrm((D, D_FF), 0.01),
        "mlp_w2": nrm((D, D_FF, D_MODEL), DEEPNORM_BETA * D_FF ** -0.5),
        "mlp_b2": nrm((D, D_MODEL), 0.01),
        "ln2_g": gain((D, D_MODEL)),
        "ln2_b": nrm((D, D_MODEL), 0.01),
    }


def reference(x_prompt, x_sample, c_prompt, c_sample, w_ada, b_ada, w_in,
              conv_dw_w, conv_dw_b, conv_ln_g, conv_ln_b, conv_pw_w,
              hy_short_w, hy_short_b, hy_ffn_w1, hy_ffn_b1, hy_sin_freq, hy_ffn_w2, hy_ffn_b2,
              hy_ffn_w3, hy_skip, hy_out_w, w_out, ln1_g, ln1_b,
              mlp_w1, mlp_b1, mlp_w2, mlp_b2, ln2_g, ln2_b):
    params = (w_ada, b_ada, w_in, conv_dw_w, conv_dw_b, conv_ln_g, conv_ln_b, conv_pw_w,
              hy_short_w, hy_short_b, hy_ffn_w1, hy_ffn_b1, hy_sin_freq, hy_ffn_w2, hy_ffn_b2,
              hy_ffn_w3, hy_skip, hy_out_w, w_out, ln1_g, ln1_b,
              mlp_w1, mlp_b1, mlp_w2, mlp_b2, ln2_g, ln2_b)
    y_prompt = _encoder(x_prompt, c_prompt, params)
    y_sample = _encoder(x_sample, c_sample, params)
    return (y_prompt, y_sample)
```

```python
import functools
import math

import jax
import jax.numpy as jnp
from jax import lax
from jax.experimental import pallas as pl
from jax.experimental.pallas import tpu as pltpu

F32 = jnp.float32
BF16 = jnp.bfloat16
HIGHEST = lax.Precision.HIGHEST

LANES = 128
SUBLANES = 8
MXU_DIM = 256
VMEM_LIMIT_BYTES = 56 * 1024 * 1024

D_MODEL = 1024
D_CONV = D_MODEL // 2
D_HYENA = D_MODEL // 2
CONV_KERNEL = 31
SHORT_KERNEL = 3
HYENA_ORDER = 2
N_BANDS = 16
FILTER_EMB = 1 + 2 * N_BANDS
FILTER_HIDDEN = 64
N_DIRS = 2
D_FF = 4 * D_MODEL
N_MOD = 6
LN_EPS = 1e-5
DEPTH = 1
MAX_DECAY = math.log(1e-2) / 0.3
MIN_DECAY = math.log(1e-2) / 1.5
DEEPNORM_ALPHA = (2.0 * DEPTH) ** 0.25

CONV_HALO = 16
TOEP = MXU_DIM


def _tiles(seq_len):
    return dict(
        tt_a=256,
        tt_c=256,
        ch_b=8,
        filt_rows=128,
        nb=seq_len // TOEP,
        width=2 * seq_len + TOEP,
        slabs=(2 * seq_len + TOEP) // LANES,
        slab_pitch=-(-((2 * seq_len + TOEP) // LANES) // SUBLANES) * SUBLANES,
    )


def _cparams(sem):
    return pltpu.CompilerParams(dimension_semantics=sem, vmem_limit_bytes=VMEM_LIMIT_BYTES)


def _ln_rows(x):
    mu = jnp.mean(x, axis=-1, keepdims=True)
    xc = x - mu
    var = jnp.mean(xc * xc, axis=-1, keepdims=True)
    return xc * lax.rsqrt(var + LN_EPS)


def _mod_kernel(c_ref, w_ref, b_ref, o_ref):
    c = c_ref[...]
    s = c * jax.nn.sigmoid(c)
    o_ref[...] = jnp.dot(s, w_ref[...], precision=HIGHEST, preferred_element_type=F32) + b_ref[...]


def _mod(c_all, w_ada, b_ada):
    rows = c_all.shape[0]
    n = w_ada.shape[1]
    bn = D_MODEL
    return pl.pallas_call(
        _mod_kernel,
        grid=(n // bn,),
        in_specs=[pl.BlockSpec((rows, D_MODEL), lambda j: (0, 0)),
                  pl.BlockSpec((D_MODEL, bn), lambda j: (0, j)),
                  pl.BlockSpec((1, bn), lambda j: (0, j))],
        out_specs=pl.BlockSpec((rows, bn), lambda j: (0, j)),
        out_shape=jax.ShapeDtypeStruct((rows, n), F32),
        compiler_params=_cparams(("parallel",)),
        name="adaln_mod",
    )(c_all, w_ada, b_ada.reshape(1, n))


def _filter_kernel(w1t_ref, b1_ref, fr_ref, w2t_ref, b2_ref, w3t_ref, o_ref, hdn_ref, *, seq_len, cfg):
    width, slabs, pitch, rows = cfg["width"], cfg["slabs"], cfg["slab_pitch"], cfg["filt_rows"]
    centre = seq_len + TOEP - 1
    m = lax.broadcasted_iota(jnp.int32, (1, width), 1)
    tau = jnp.abs(m - centre).astype(F32)
    t = tau * (1.0 / (seq_len - 1))

    @pl.when(pl.program_id(0) == 0)
    def _():
        w = tau * (2.0 * math.pi / seq_len)
        bidx = lax.broadcasted_iota(jnp.int32, (N_BANDS, 1), 0).astype(F32)
        bands = 1e-4 + bidx * ((N_BANDS - 1 - 1e-4) / (N_BANDS - 1))
        ang = bands * w
        z = jnp.concatenate([t, jnp.cos(ang), -jnp.sin(ang)], axis=0)
        fr = fr_ref[...]
        h1 = jnp.sin(fr * (jnp.dot(w1t_ref[...], z, precision=HIGHEST, preferred_element_type=F32) + b1_ref[...]))
        hdn_ref[...] = jnp.sin(fr * (jnp.dot(w2t_ref[...], h1, precision=HIGHEST, preferred_element_type=F32)
                                     + b2_ref[...]))

    c0 = pl.program_id(0) * rows
    cidx = (lax.broadcasted_iota(jnp.int32, (rows, 1), 0) + c0).astype(F32)
    step = (MAX_DECAY - MIN_DECAY) / (D_HYENA - 1)
    delta_f = jnp.abs(MIN_DECAY + cidx * step)
    delta_b = jnp.abs(MIN_DECAY + (D_HYENA - 1 - cidx) * step)
    fwd = m >= centre
    decay = jnp.exp(-t * jnp.where(fwd, delta_f, delta_b))
    valid = tau <= float(seq_len - 1)
    hdn = hdn_ref[...]
    for o in range(HYENA_ORDER):
        hf = jnp.dot(w3t_ref[o, 0], hdn, precision=HIGHEST, preferred_element_type=F32)
        hb = jnp.dot(w3t_ref[o, 1], hdn, precision=HIGHEST, preferred_element_type=F32)
        k = jnp.where(valid, jnp.where(fwd, hf, hb) * decay, 0.0)
        for s in range(slabs):
            o_ref[o, pl.ds(s, rows, stride=pitch), :] = k[:, s * LANES:(s + 1) * LANES]
        for s in range(slabs, pitch):
            o_ref[o, pl.ds(s, rows, stride=pitch), :] = jnp.zeros((rows, LANES), F32)


def _filters(seq_len, w1, b1, freq, w2, b2, w3):
    cfg = _tiles(seq_len)
    rows, pitch, width = cfg["filt_rows"], cfg["slab_pitch"], cfg["width"]
    w3t = w3.T.reshape(HYENA_ORDER, N_DIRS, D_HYENA, FILTER_HIDDEN)
    col = lambda v: v.reshape(FILTER_HIDDEN, 1)
    full = lambda shape: pl.BlockSpec(shape, lambda i: (0,) * len(shape))
    return pl.pallas_call(
        functools.partial(_filter_kernel, seq_len=seq_len, cfg=cfg),
        grid=(D_HYENA // rows,),
        in_specs=[full((FILTER_HIDDEN, FILTER_EMB)), full((FILTER_HIDDEN, 1)), full((FILTER_HIDDEN, 1)),
                  full((FILTER_HIDDEN, FILTER_HIDDEN)), full((FILTER_HIDDEN, 1)),
                  pl.BlockSpec((HYENA_ORDER, N_DIRS, rows, FILTER_HIDDEN), lambda i: (0, 0, i, 0))],
        out_specs=pl.BlockSpec((HYENA_ORDER, rows * pitch, LANES), lambda i: (0, i, 0)),
        out_shape=jax.ShapeDtypeStruct((HYENA_ORDER, D_HYENA * pitch, LANES), F32),
        scratch_shapes=[pltpu.VMEM((FILTER_HIDDEN, width), F32)],
        compiler_params=_cparams(("arbitrary",)),
        name=f"hyena_filters_{seq_len}",
    )(w1.T, col(b1), col(freq), w2.T, col(b2), w3t)


def _stage_a_kernel(x_ref, xp_ref, xn_ref, mod_ref, wa_ref, whyt_ref, wg_ref, dww_ref, dwb_ref,
                    lng_ref, lnb_ref, wpw_ref, gy_ref, sgb_ref, hyt_ref, *, seq_len, tt):
    i = pl.program_id(1)
    ext = tt + 2 * CONV_HALO
    xe = jnp.concatenate([xp_ref[0], x_ref[0], xn_ref[0]], axis=0)
    shift1 = mod_ref[0, 0:1, :]
    scale1 = mod_ref[0, 1:2, :]
    h = (_ln_rows(xe) * (1.0 + scale1) + shift1).astype(BF16)
    hm = h[CONV_HALO:CONV_HALO + tt]

    pa = jnp.dot(h, wa_ref[...], preferred_element_type=F32)
    a = pa[:, :D_CONV] * jax.nn.sigmoid(pa[:, D_CONV:])
    pos = i * tt - CONV_HALO + lax.broadcasted_iota(jnp.int32, (ext, 1), 0)
    a = jnp.where((pos >= 0) & (pos < seq_len), a, 0.0)
    acc = jnp.zeros((tt, D_CONV), F32) + dwb_ref[...]
    first = CONV_HALO - CONV_KERNEL // 2
    for r in range(SUBLANES):
        taps = [k for k in range(CONV_KERNEL) if (first + k) % SUBLANES == r]
        if not taps:
            continue
        last_q = max((first + k) // SUBLANES for k in taps)
        a_r = a[r:r + last_q * SUBLANES + tt]
        for k in taps:
            q = (first + k) // SUBLANES
            acc = acc + dww_ref[k:k + 1, :] * a_r[q * SUBLANES:q * SUBLANES + tt]
    a2 = _ln_rows(acc) * lng_ref[...] + lnb_ref[...]
    a2 = a2 * jax.nn.sigmoid(a2)
    ya = jnp.dot(a2.astype(BF16), wpw_ref[...], preferred_element_type=F32)

    pg = jnp.dot(hm, wg_ref[...], preferred_element_type=F32)
    gy_ref[0] = jax.nn.sigmoid(pg[:, :D_MODEL]) * ya
    sgb_ref[0] = jax.nn.sigmoid(pg[:, D_MODEL:])

    hyt_ref[0] = lax.dot_general(whyt_ref[...], hm, (((1,), (1,)), ((), ())), preferred_element_type=F32)


def _const_spec(shape):
    return pl.BlockSpec(shape, lambda b, i: (0,) * len(shape), pipeline_mode=pl.Buffered(1))


def _stage_a(x, mod, wa, whyt, wg, dww, dwb, lng, lnb, wpw):
    bsz, seq_len, _ = x.shape
    tt = _tiles(seq_len)["tt_a"]
    nh = tt // CONV_HALO
    last_halo = seq_len // CONV_HALO - 1
    hy_rows = (HYENA_ORDER + 1) * D_HYENA
    return pl.pallas_call(
        functools.partial(_stage_a_kernel, seq_len=seq_len, tt=tt),
        grid=(bsz, seq_len // tt),
        in_specs=[
            pl.BlockSpec((1, tt, D_MODEL), lambda b, i: (b, i, 0)),
            pl.BlockSpec((1, CONV_HALO, D_MODEL), lambda b, i: (b, jnp.maximum(i * nh - 1, 0), 0)),
            pl.BlockSpec((1, CONV_HALO, D_MODEL), lambda b, i: (b, jnp.minimum((i + 1) * nh, last_halo), 0)),
            pl.BlockSpec((1, N_MOD, D_MODEL), lambda b, i: (b, 0, 0)),
            _const_spec((D_MODEL, 2 * D_CONV)),
            _const_spec((hy_rows, D_MODEL)),
            _const_spec((D_MODEL, 2 * D_MODEL)),
            _const_spec((CONV_KERNEL, D_CONV)),
            _const_spec((1, D_CONV)), _const_spec((1, D_CONV)), _const_spec((1, D_CONV)),
            _const_spec((D_CONV, D_MODEL)),
        ],
        out_specs=[
            pl.BlockSpec((1, tt, D_MODEL), lambda b, i: (b, i, 0)),
            pl.BlockSpec((1, tt, D_MODEL), lambda b, i: (b, i, 0)),
            pl.BlockSpec((1, hy_rows, tt), lambda b, i: (b, 0, i)),
        ],
        out_shape=[jax.ShapeDtypeStruct((bsz, seq_len, D_MODEL), F32),
                   jax.ShapeDtypeStruct((bsz, seq_len, D_MODEL), F32),
                   jax.ShapeDtypeStruct((bsz, hy_rows, seq_len), F32)],
        compiler_params=_cparams(("parallel", "parallel")),
        name=f"stage_a_{seq_len}",
    )(x, x, x, mod, wa, whyt, wg, dww, dwb, lng, lnb, wpw)


def _stage_b_kernel(sw_ref, sb_ref, skip_ref, filt_ref, hyt_hbm, out_hbm,
                    inbuf, outbuf, conv_ref, toep_ref, lhs_ref, zf_ref, acc_ref, in_sem, out_sem,
                    *, bsz, seq_len, cfg):
    nb, slabs, pitch, chans = cfg["nb"], cfg["slabs"], cfg["slab_pitch"], cfg["ch_b"]
    c_base = pl.program_id(0) * chans

    def in_copy(j, slot, part):
        return pltpu.make_async_copy(hyt_hbm.at[:, part * D_HYENA + c_base + j, :],
                                     inbuf.at[slot, part], in_sem.at[slot, part])

    def out_copy(j, slot):
        return pltpu.make_async_copy(outbuf.at[slot], out_hbm.at[:, c_base + j, :], out_sem.at[slot])

    lane = lax.broadcasted_iota(jnp.int32, (bsz, seq_len), 1)
    row = lax.broadcasted_iota(jnp.int32, (LANES, LANES), 0)
    col = lax.broadcasted_iota(jnp.int32, (LANES, LANES), 1)
    low = col <= row

    def build_toeplitz(o, j):
        prev = None
        for k in range(slabs):
            fb = jnp.broadcast_to(filt_ref[o, pl.ds(j * pitch + k, 1), :], (LANES, LANES))
            cur = pltpu.roll(fb, 1, 1, stride=1, stride_axis=0)
            if prev is not None:
                big = jnp.where(low, prev, cur).astype(BF16)
                if k - 1 <= slabs - 3:
                    toep_ref[LANES:, (k - 1) * LANES:k * LANES] = big
                if k - 1 >= 1:
                    toep_ref[:LANES, (k - 2) * LANES:(k - 1) * LANES] = big
            prev = cur

    def long_conv():
        acc_ref[...] = jnp.dot(lhs_ref[...], toep_ref[:, nb * TOEP:(nb + 1) * TOEP], preferred_element_type=F32)
        for d in range(1, nb):
            keep = (nb - d) * bsz
            acc_ref[d * bsz:, :] += jnp.dot(lhs_ref[:keep, :], toep_ref[:, (nb + d) * TOEP:(nb + d + 1) * TOEP],
                                            preferred_element_type=F32)
            acc_ref[:keep, :] += jnp.dot(lhs_ref[d * bsz:, :], toep_ref[:, (nb - d) * TOEP:(nb - d + 1) * TOEP],
                                         preferred_element_type=F32)

    def channel(j, carry):
        slot = j % 2
        c = c_base + j
        for part in range(HYENA_ORDER + 1):
            in_copy(j, slot, part).wait()

        @pl.when(j + 1 < chans)
        def _():
            for part in range(HYENA_ORDER + 1):
                in_copy(j + 1, 1 - slot, part).start()

        for part in range(HYENA_ORDER + 1):
            p = inbuf[slot, part]
            ch = part * D_HYENA + c
            before = jnp.where(lane == 0, 0.0, pltpu.roll(p, 1, 1))
            after = jnp.where(lane == seq_len - 1, 0.0, pltpu.roll(p, seq_len - 1, 1))
            conv_ref[part] = (sw_ref[ch] * before + sw_ref[3 * D_HYENA + ch] * p
                              + sw_ref[6 * D_HYENA + ch] * after + sb_ref[ch])

        for si in range(nb):
            lhs_ref[si * bsz:(si + 1) * bsz, :] = conv_ref[0, :, si * TOEP:(si + 1) * TOEP].astype(BF16)
        build_toeplitz(0, j)
        long_conv()
        skip0 = skip_ref[c]
        for si in range(nb):
            rows = slice(si * bsz, (si + 1) * bsz)
            cols = slice(si * TOEP, (si + 1) * TOEP)
            z = conv_ref[1, :, cols] * (acc_ref[rows, :] + conv_ref[0, :, cols] * skip0)
            zf_ref[rows, :] = z
            lhs_ref[rows, :] = z.astype(BF16)
        build_toeplitz(1, j)
        long_conv()
        skip1 = skip_ref[D_HYENA + c]

        @pl.when(j >= 2)
        def _():
            out_copy(j - 2, slot).wait()

        for si in range(nb):
            rows = slice(si * bsz, (si + 1) * bsz)
            cols = slice(si * TOEP, (si + 1) * TOEP)
            outbuf[slot, :, cols] = conv_ref[2, :, cols] * (acc_ref[rows, :] + zf_ref[rows, :] * skip1)
        out_copy(j, slot).start()
        return carry

    for part in range(HYENA_ORDER + 1):
        in_copy(0, 0, part).start()
    lax.fori_loop(0, chans, channel, 0)
    out_copy(chans - 2, chans % 2).wait()
    out_copy(chans - 1, (chans - 1) % 2).wait()


def _stage_b(hyt, filt, short_w, short_b, skip):
    bsz, _, seq_len = hyt.shape
    cfg = _tiles(seq_len)
    chans, pitch, nb = cfg["ch_b"], cfg["slab_pitch"], cfg["nb"]
    smem = pl.BlockSpec(memory_space=pltpu.SMEM)
    return pl.pallas_call(
        functools.partial(_stage_b_kernel, bsz=bsz, seq_len=seq_len, cfg=cfg),
        grid=(D_HYENA // chans,),
        in_specs=[smem, smem, smem,
                  pl.BlockSpec((HYENA_ORDER, chans * pitch, LANES), lambda g: (0, g, 0)),
                  pl.BlockSpec(memory_space=pl.ANY)],
        out_specs=pl.BlockSpec(memory_space=pl.ANY),
        out_shape=jax.ShapeDtypeStruct((bsz, D_HYENA, seq_len), F32),
        scratch_shapes=[
            pltpu.VMEM((2, HYENA_ORDER + 1, bsz, seq_len), F32),
            pltpu.VMEM((2, bsz, seq_len), F32),
            pltpu.VMEM((HYENA_ORDER + 1, bsz, seq_len), F32),
            pltpu.VMEM((TOEP, 2 * seq_len), BF16),
            pltpu.VMEM((nb * bsz, TOEP), BF16),
            pltpu.VMEM((nb * bsz, TOEP), F32),
            pltpu.VMEM((nb * bsz, TOEP), F32),
            pltpu.SemaphoreType.DMA((2, HYENA_ORDER + 1)),
            pltpu.SemaphoreType.DMA((2,)),
        ],
        compiler_params=_cparams(("arbitrary",)),
        name=f"stage_b_{seq_len}",
    )(short_w.reshape(-1), short_b.reshape(-1), skip.reshape(-1), filt, hyt)


def _stage_c_kernel(x_ref, gy_ref, sgb_ref, zbt_ref, mod_ref, whyo_ref, wout_ref, ln1g_ref, ln1b_ref,
                    w1_ref, b1_ref, w2_ref, b2_ref, ln2g_ref, ln2b_ref, o_ref, *, ff_chunk):
    gate1 = mod_ref[0, 2:3, :]
    shift2 = mod_ref[0, 3:4, :]
    scale2 = mod_ref[0, 4:5, :]
    gate2 = mod_ref[0, 5:6, :]
    zb = zbt_ref[0].T.astype(BF16)
    yb = jnp.dot(zb, whyo_ref[...], preferred_element_type=F32)
    m = gy_ref[0] + sgb_ref[0] * yb
    mix = jnp.dot(m.astype(BF16), wout_ref[...], preferred_element_type=F32)
    x1 = _ln_rows(DEEPNORM_ALPHA * x_ref[0] + gate1 * mix) * ln1g_ref[...] + ln1b_ref[...]
    h2 = (_ln_rows(x1) * (1.0 + scale2) + shift2).astype(BF16)
    f = jnp.zeros(x1.shape, F32) + b2_ref[...]
    for j in range(D_FF // ff_chunk):
        cols = slice(j * ff_chunk, (j + 1) * ff_chunk)
        u = jnp.maximum(jnp.dot(h2, w1_ref[:, cols], preferred_element_type=F32) + b1_ref[:, cols], 0.0)
        f = f + jnp.dot((u * u).astype(BF16), w2_ref[cols, :], preferred_element_type=F32)
    o_ref[0] = _ln_rows(DEEPNORM_ALPHA * x1 + gate2 * f) * ln2g_ref[...] + ln2b_ref[...]


def _stage_c(x, gy, sgb, zbt, mod, whyo, wout, ln1g, ln1b, w1, b1, w2, b2, ln2g, ln2b):
    bsz, seq_len, _ = x.shape
    tt = _tiles(seq_len)["tt_c"]
    tok = pl.BlockSpec((1, tt, D_MODEL), lambda b, i: (b, i, 0))
    return pl.pallas_call(
        functools.partial(_stage_c_kernel, ff_chunk=D_MODEL),
        grid=(bsz, seq_len // tt),
        in_specs=[tok, tok, tok,
                  pl.BlockSpec((1, D_HYENA, tt), lambda b, i: (b, 0, i)),
                  pl.BlockSpec((1, N_MOD, D_MODEL), lambda b, i: (b, 0, 0)),
                  _const_spec((D_HYENA, D_MODEL)), _const_spec((D_MODEL, D_MODEL)),
                  _const_spec((1, D_MODEL)), _const_spec((1, D_MODEL)),
                  _const_spec((D_MODEL, D_FF)), _const_spec((1, D_FF)),
                  _const_spec((D_FF, D_MODEL)), _const_spec((1, D_MODEL)),
                  _const_spec((1, D_MODEL)), _const_spec((1, D_MODEL))],
        out_specs=tok,
        out_shape=jax.ShapeDtypeStruct((bsz, seq_len, D_MODEL), F32),
        compiler_params=_cparams(("parallel", "parallel")),
        name=f"stage_c_{seq_len}",
    )(x, gy, sgb, zbt, mod, whyo, wout, ln1g, ln1b, w1, b1, w2, b2, ln2g, ln2b)


def _encoder(x, mod, filt, p):
    gy, sgb, hyt = _stage_a(x, mod, p["wa"], p["whyt"], p["wg"], p["dww"], p["dwb"], p["lng"], p["lnb"], p["wpw"])
    zbt = _stage_b(hyt, filt, p["short_w"], p["short_b"], p["skip"])
    return _stage_c(x, gy, sgb, zbt, mod, p["whyo"], p["wout"], p["ln1g"], p["ln1b"],
                    p["w1"], p["b1"], p["w2"], p["b2"], p["ln2g"], p["ln2b"])


def kernel(x_prompt, x_sample, c_prompt, c_sample, w_ada, b_ada, w_in, conv_dw_w, conv_dw_b, conv_ln_g, conv_ln_b, conv_pw_w, hy_short_w, hy_short_b, hy_ffn_w1, hy_ffn_b1, hy_sin_freq, hy_ffn_w2, hy_ffn_b2, hy_ffn_w3, hy_skip, hy_out_w, w_out, ln1_g, ln1_b, mlp_w1, mlp_b1, mlp_w2, mlp_b2, ln2_g, ln2_b):
    l = 0
    row = lambda v: v.reshape(1, -1)
    s1 = 2 * D_CONV
    s2 = s1 + (HYENA_ORDER + 1) * D_HYENA
    w_in_l = w_in[l]
    p = dict(
        wa=w_in_l[:, :s1].astype(BF16),
        whyt=w_in_l[:, s1:s2].T.astype(BF16),
        wg=w_in_l[:, s2:].astype(BF16),
        dww=conv_dw_w[l], dwb=row(conv_dw_b[l]), lng=row(conv_ln_g[l]), lnb=row(conv_ln_b[l]),
        wpw=conv_pw_w[l].astype(BF16),
        short_w=hy_short_w[l], short_b=hy_short_b[l], skip=hy_skip[l],
        whyo=hy_out_w[l].astype(BF16), wout=w_out[l].astype(BF16),
        ln1g=row(ln1_g[l]), ln1b=row(ln1_b[l]),
        w1=mlp_w1[l].astype(BF16), b1=row(mlp_b1[l]), w2=mlp_w2[l].astype(BF16), b2=row(mlp_b2[l]),
        ln2g=row(ln2_g[l]), ln2b=row(ln2_b[l]),
    )
    nb_p = x_prompt.shape[0]
    mod = _mod(jnp.concatenate([c_prompt, c_sample], axis=0), w_ada[l], b_ada[l])
    mod = mod.reshape(-1, N_MOD, D_MODEL)
    outs = []
    for x, m in ((x_prompt, mod[:nb_p]), (x_sample, mod[nb_p:])):
        filt = _filters(x.shape[1], hy_ffn_w1[l], hy_ffn_b1[l], hy_sin_freq[l],
                        hy_ffn_w2[l], hy_ffn_b2[l], hy_ffn_w3[l])
        outs.append(_encoder(x, m, filt, p))
    return tuple(outs)
```

```python
import functools
import math

import jax
import jax.numpy as jnp
from jax import lax
from jax.experimental import pallas as pl
from jax.experimental.pallas import tpu as pltpu

F32 = jnp.float32
BF16 = jnp.bfloat16
HIGHEST = lax.Precision.HIGHEST

LANES = 128
SUBLANES = 8
MXU_DIM = 256
VMEM_LIMIT_BYTES = 56 * 1024 * 1024

D_MODEL = 1024
D_CONV = D_MODEL // 2
D_HYENA = D_MODEL // 2
CONV_KERNEL = 31
SHORT_KERNEL = 3
HYENA_ORDER = 2
N_BANDS = 16
FILTER_EMB = 1 + 2 * N_BANDS
FILTER_HIDDEN = 64
N_DIRS = 2
D_FF = 4 * D_MODEL
N_MOD = 6
LN_EPS = 1e-5
DEPTH = 1
MAX_DECAY = math.log(1e-2) / 0.3
MIN_DECAY = math.log(1e-2) / 1.5
DEEPNORM_ALPHA = (2.0 * DEPTH) ** 0.25

CONV_HALO = 16
TOEP = MXU_DIM


def _tiles(seq_len):
    return dict(
        tt_a=512,
        tt_c=512,
        ch_b=32,
        filt_rows=128,
        nb=seq_len // TOEP,
        width=2 * seq_len + TOEP,
        slabs=(2 * seq_len + TOEP) // LANES,
        slab_pitch=-(-((2 * seq_len + TOEP) // LANES) // SUBLANES) * SUBLANES,
    )


def _cparams(sem):
    return pltpu.CompilerParams(dimension_semantics=sem, vmem_limit_bytes=VMEM_LIMIT_BYTES)


def _ln_rows(x):
    mu = jnp.mean(x, axis=-1, keepdims=True)
    xc = x - mu
    var = jnp.mean(xc * xc, axis=-1, keepdims=True)
    return xc * lax.rsqrt(var + LN_EPS)


def _mod_kernel(c_ref, w_ref, b_ref, o_ref):
    c = c_ref[...]
    s = c * jax.nn.sigmoid(c)
    o_ref[...] = jnp.dot(s, w_ref[...], precision=HIGHEST, preferred_element_type=F32) + b_ref[...]


def _mod(c_all, w_ada, b_ada):
    rows = c_all.shape[0]
    n = w_ada.shape[1]
    bn = D_MODEL
    return pl.pallas_call(
        _mod_kernel,
        grid=(n // bn,),
        in_specs=[pl.BlockSpec((rows, D_MODEL), lambda j: (0, 0)),
                  pl.BlockSpec((D_MODEL, bn), lambda j: (0, j)),
                  pl.BlockSpec((1, bn), lambda j: (0, j))],
        out_specs=pl.BlockSpec((rows, bn), lambda j: (0, j)),
        out_shape=jax.ShapeDtypeStruct((rows, n), F32),
        compiler_params=_cparams(("parallel",)),
        name="adaln_mod",
    )(c_all, w_ada, b_ada.reshape(1, n))


def _bf16_bits(x):
    b = lax.bitcast_convert_type(x, jnp.uint32)
    return (b + jnp.uint32(0x7FFF) + ((b >> 16) & jnp.uint32(1))) >> 16


def _filter_kernel(w1t_ref, b1_ref, fr_ref, w2t_ref, b2_ref, w3t_ref, o_ref, hdn_ref, *, seq_len, cfg):
    width, slabs, pitch, rows = cfg["width"], cfg["slabs"], cfg["slab_pitch"], cfg["filt_rows"]
    centre = seq_len + TOEP - 1
    m = lax.broadcasted_iota(jnp.int32, (1, width), 1)
    tau = jnp.abs(m - centre).astype(F32)
    t = tau * (1.0 / (seq_len - 1))

    @pl.when(pl.program_id(0) == 0)
    def _():
        w = tau * (2.0 * math.pi / seq_len)
        bidx = lax.broadcasted_iota(jnp.int32, (N_BANDS, 1), 0).astype(F32)
        bands = 1e-4 + bidx * ((N_BANDS - 1 - 1e-4) / (N_BANDS - 1))
        ang = bands * w
        z = jnp.concatenate([t, jnp.cos(ang), -jnp.sin(ang)], axis=0)
        fr = fr_ref[...]
        h1 = jnp.sin(fr * (jnp.dot(w1t_ref[...], z, precision=HIGHEST, preferred_element_type=F32) + b1_ref[...]))
        hdn_ref[...] = jnp.sin(fr * (jnp.dot(w2t_ref[...], h1, precision=HIGHEST, preferred_element_type=F32)
                                     + b2_ref[...]))

    c0 = pl.program_id(0) * rows
    cidx = (lax.broadcasted_iota(jnp.int32, (rows, 1), 0) + c0).astype(F32)
    step = (MAX_DECAY - MIN_DECAY) / (D_HYENA - 1)
    delta_f = jnp.abs(MIN_DECAY + cidx * step)
    delta_b = jnp.abs(MIN_DECAY + (D_HYENA - 1 - cidx) * step)
    fwd = m >= centre
    decay = jnp.exp(-t * jnp.where(fwd, delta_f, delta_b))
    valid = tau <= float(seq_len - 1)
    hdn = hdn_ref[...]
    for o in range(HYENA_ORDER):
        hf = jnp.dot(w3t_ref[o, 0], hdn, precision=HIGHEST, preferred_element_type=F32)
        hb = jnp.dot(w3t_ref[o, 1], hdn, precision=HIGHEST, preferred_element_type=F32)
        k = jnp.where(valid, jnp.where(fwd, hf, hb) * decay, 0.0)
        word = _bf16_bits(k) | (_bf16_bits(pltpu.roll(k, 1, 1)) << 16)
        for s in range(slabs):
            o_ref[o, pl.ds(s, rows, stride=pitch), :] = word[:, s * LANES:(s + 1) * LANES]
        for s in range(slabs, pitch):
            o_ref[o, pl.ds(s, rows, stride=pitch), :] = jnp.zeros((rows, LANES), jnp.uint32)


def _filters(seq_len, w1, b1, freq, w2, b2, w3):
    cfg = _tiles(seq_len)
    rows, pitch, width = cfg["filt_rows"], cfg["slab_pitch"], cfg["width"]
    w3t = w3.T.reshape(HYENA_ORDER, N_DIRS, D_HYENA, FILTER_HIDDEN)
    col = lambda v: v.reshape(FILTER_HIDDEN, 1)
    full = lambda shape: pl.BlockSpec(shape, lambda i: (0,) * len(shape))
    return pl.pallas_call(
        functools.partial(_filter_kernel, seq_len=seq_len, cfg=cfg),
        grid=(D_HYENA // rows,),
        in_specs=[full((FILTER_HIDDEN, FILTER_EMB)), full((FILTER_HIDDEN, 1)), full((FILTER_HIDDEN, 1)),
                  full((FILTER_HIDDEN, FILTER_HIDDEN)), full((FILTER_HIDDEN, 1)),
                  pl.BlockSpec((HYENA_ORDER, N_DIRS, rows, FILTER_HIDDEN), lambda i: (0, 0, i, 0))],
        out_specs=pl.BlockSpec((HYENA_ORDER, rows * pitch, LANES), lambda i: (0, i, 0)),
        out_shape=jax.ShapeDtypeStruct((HYENA_ORDER, D_HYENA * pitch, LANES), jnp.uint32),
        scratch_shapes=[pltpu.VMEM((FILTER_HIDDEN, width), F32)],
        compiler_params=_cparams(("arbitrary",)),
        name=f"hyena_filters_{seq_len}",
    )(w1.T, col(b1), col(freq), w2.T, col(b2), w3t)


def _stage_a_kernel(x_ref, xp_ref, xn_ref, mod_ref, wa_ref, whyt_ref, wg_ref, dww_ref, dwb_ref,
                    lng_ref, lnb_ref, wpw_ref, gy_ref, sgb_ref, hyt_ref, a_ref, shift_ref, *, seq_len, tt):
    i = pl.program_id(1)
    ext = tt + 2 * CONV_HALO
    shift_rows = shift_ref.shape[1]
    xe = jnp.concatenate([xp_ref[0], x_ref[0], xn_ref[0]], axis=0)
    shift1 = mod_ref[0, 0:1, :]
    scale1 = mod_ref[0, 1:2, :]
    h = (_ln_rows(xe) * (1.0 + scale1) + shift1).astype(BF16)
    hm = h[CONV_HALO:CONV_HALO + tt]

    pa = jnp.dot(h, wa_ref[...], preferred_element_type=F32)
    a = pa[:, :D_CONV] * jax.nn.sigmoid(pa[:, D_CONV:])
    pos = i * tt - CONV_HALO + lax.broadcasted_iota(jnp.int32, (ext, 1), 0)
    a_ref[...] = jnp.where((pos >= 0) & (pos < seq_len), a, 0.0)
    for r in range(1, SUBLANES):
        shift_ref[r - 1] = a_ref[r:r + shift_rows, :]
    acc = jnp.zeros((tt, D_CONV), F32) + dwb_ref[...]
    first = CONV_HALO - CONV_KERNEL // 2
    for k in range(CONV_KERNEL):
        q, r = divmod(first + k, SUBLANES)
        rows = pl.ds(q * SUBLANES, tt)
        tap = a_ref[rows, :] if r == 0 else shift_ref[r - 1, rows, :]
        acc = acc + dww_ref[k:k + 1, :] * tap
    a2 = _ln_rows(acc) * lng_ref[...] + lnb_ref[...]
    a2 = a2 * jax.nn.sigmoid(a2)
    ya = jnp.dot(a2.astype(BF16), wpw_ref[...], preferred_element_type=F32)

    pg = jnp.dot(hm, wg_ref[...], preferred_element_type=F32)
    gy_ref[0] = jax.nn.sigmoid(pg[:, :D_MODEL]) * ya
    sgb_ref[0] = jax.nn.sigmoid(pg[:, D_MODEL:])

    hyt_ref[0] = lax.dot_general(whyt_ref[...], hm, (((1,), (1,)), ((), ())), preferred_element_type=F32)


def _const_spec(shape):
    return pl.BlockSpec(shape, lambda b, i: (0,) * len(shape), pipeline_mode=pl.Buffered(1))


def _stage_a(x, mod, wa, whyt, wg, dww, dwb, lng, lnb, wpw):
    bsz, seq_len, _ = x.shape
    tt = _tiles(seq_len)["tt_a"]
    nh = tt // CONV_HALO
    last_halo = seq_len // CONV_HALO - 1
    hy_rows = (HYENA_ORDER + 1) * D_HYENA
    return pl.pallas_call(
        functools.partial(_stage_a_kernel, seq_len=seq_len, tt=tt),
        grid=(bsz, seq_len // tt),
        in_specs=[
            pl.BlockSpec((1, tt, D_MODEL), lambda b, i: (b, i, 0)),
            pl.BlockSpec((1, CONV_HALO, D_MODEL), lambda b, i: (b, jnp.maximum(i * nh - 1, 0), 0)),
            pl.BlockSpec((1, CONV_HALO, D_MODEL), lambda b, i: (b, jnp.minimum((i + 1) * nh, last_halo), 0)),
            pl.BlockSpec((1, N_MOD, D_MODEL), lambda b, i: (b, 0, 0)),
            _const_spec((D_MODEL, 2 * D_CONV)),
            _const_spec((hy_rows, D_MODEL)),
            _const_spec((D_MODEL, 2 * D_MODEL)),
            _const_spec((CONV_KERNEL, D_CONV)),
            _const_spec((1, D_CONV)), _const_spec((1, D_CONV)), _const_spec((1, D_CONV)),
            _const_spec((D_CONV, D_MODEL)),
        ],
        out_specs=[
            pl.BlockSpec((1, tt, D_MODEL), lambda b, i: (b, i, 0)),
            pl.BlockSpec((1, tt, D_MODEL), lambda b, i: (b, i, 0)),
            pl.BlockSpec((1, hy_rows, tt), lambda b, i: (b, 0, i)),
        ],
        out_shape=[jax.ShapeDtypeStruct((bsz, seq_len, D_MODEL), F32),
                   jax.ShapeDtypeStruct((bsz, seq_len, D_MODEL), F32),
                   jax.ShapeDtypeStruct((bsz, hy_rows, seq_len), F32)],
        scratch_shapes=[
            pltpu.VMEM((tt + 2 * CONV_HALO, D_CONV), F32),
            pltpu.VMEM((SUBLANES - 1, tt + 2 * CONV_HALO - SUBLANES, D_CONV), F32),
        ],
        compiler_params=_cparams(("parallel", "parallel")),
        name=f"stage_a_{seq_len}",
    )(x, x, x, mod, wa, whyt, wg, dww, dwb, lng, lnb, wpw)


def _stage_b_kernel(sw_ref, sb_ref, skip_ref, filt_ref, hyt_hbm, out_hbm,
                    inbuf, outbuf, conv_ref, toep_a, toep_b, lhs_ref, zf_ref, acc_ref, in_sem, out_sem,
                    *, bsz, seq_len, cfg):
    nb, slabs, pitch, chans = cfg["nb"], cfg["slabs"], cfg["slab_pitch"], cfg["ch_b"]
    parts = HYENA_ORDER + 1
    pairs = chans // 2
    c_base = pl.program_id(0) * chans

    def in_copy(j, slot, part):
        return pltpu.make_async_copy(hyt_hbm.at[:, part * D_HYENA + c_base + j, :],
                                     inbuf.at[slot, part], in_sem.at[slot, part])

    def out_copy(j, slot):
        return pltpu.make_async_copy(outbuf.at[slot], out_hbm.at[:, c_base + j, :], out_sem.at[slot])

    lane = lax.broadcasted_iota(jnp.int32, (bsz, seq_len), 1)
    half = LANES // 2
    word_row = lax.broadcasted_iota(jnp.int32, (half, LANES), 0)
    col = lax.broadcasted_iota(jnp.int32, (half, LANES), 1)
    low = col <= 2 * word_row

    def build_toeplitz(toep_ref, j):
        for o in range(HYENA_ORDER):
            prev = None
            for k in range(slabs):
                fb = jnp.broadcast_to(filt_ref[o, pl.ds(j * pitch + k, 1), :], (half, LANES))
                cur = pltpu.roll(fb, 1, 1, stride=2, stride_axis=0)
                if prev is not None:
                    big = pltpu.bitcast(jnp.where(low, prev, cur), BF16)
                    if k - 1 <= slabs - 3:
                        toep_ref[o, LANES:, (k - 1) * LANES:k * LANES] = big
                    if k - 1 >= 1:
                        toep_ref[o, :LANES, (k - 2) * LANES:(k - 1) * LANES] = big
                prev = cur

    def long_conv(toep_ref, o, h):
        def tile(d):
            return toep_ref[o, :, (nb + d) * TOEP:(nb + d + 1) * TOEP]
        acc_ref[h] = jnp.dot(lhs_ref[h], tile(0), preferred_element_type=F32)
        for d in range(1, nb):
            keep = (nb - d) * bsz
            acc_ref[h, d * bsz:, :] += jnp.dot(lhs_ref[h, :keep, :], tile(d), preferred_element_type=F32)
            acc_ref[h, :keep, :] += jnp.dot(lhs_ref[h, d * bsz:, :], tile(-d), preferred_element_type=F32)

    def channel(j, h, slot, toep_ref):
        c = c_base + j
        for part in range(parts):
            p = inbuf[slot, part]
            ch = part * D_HYENA + c
            before = jnp.where(lane == 0, 0.0, pltpu.roll(p, 1, 1))
            after = jnp.where(lane == seq_len - 1, 0.0, pltpu.roll(p, seq_len - 1, 1))
            conv_ref[h, part] = (sw_ref[ch] * before + sw_ref[parts * D_HYENA + ch] * p
                                 + sw_ref[2 * parts * D_HYENA + ch] * after + sb_ref[ch])
        for si in range(nb):
            lhs_ref[h, si * bsz:(si + 1) * bsz, :] = conv_ref[h, 0, :, si * TOEP:(si + 1) * TOEP].astype(BF16)
        long_conv(toep_ref, 0, h)
        skip0 = skip_ref[c]
        for si in range(nb):
            rows = slice(si * bsz, (si + 1) * bsz)
            cols = slice(si * TOEP, (si + 1) * TOEP)
            z = conv_ref[h, 1, :, cols] * (acc_ref[h, rows, :] + conv_ref[h, 0, :, cols] * skip0)
            zf_ref[h, rows, :] = z
            lhs_ref[h, rows, :] = z.astype(BF16)
        long_conv(toep_ref, 1, h)
        skip1 = skip_ref[D_HYENA + c]
        for si in range(nb):
            rows = slice(si * bsz, (si + 1) * bsz)
            cols = slice(si * TOEP, (si + 1) * TOEP)
            outbuf[slot, :, cols] = conv_ref[h, 2, :, cols] * (acc_ref[h, rows, :] + zf_ref[h, rows, :] * skip1)
        out_copy(j, slot).start()

    def pair(i, carry):
        j0 = 2 * i
        s0 = 2 * (i % 2)
        for h in range(2):
            for part in range(parts):
                in_copy(j0 + h, s0 + h, part).wait()

        @pl.when(i + 1 < pairs)
        def _():
            for h in range(2):
                for part in range(parts):
                    in_copy(j0 + 2 + h, 2 - s0 + h, part).start()

        @pl.when(i >= 2)
        def _():
            for h in range(2):
                out_copy(j0 - 4 + h, s0 + h).wait()

        build_toeplitz(toep_b, j0 + 1)
        channel(j0, 0, s0, toep_a)
        build_toeplitz(toep_a, jnp.minimum(j0 + 2, chans - 1))
        channel(j0 + 1, 1, s0 + 1, toep_b)
        return carry

    for h in range(2):
        for part in range(parts):
            in_copy(h, h, part).start()
    build_toeplitz(toep_a, 0)
    lax.fori_loop(0, pairs, pair, 0)
    for i in range(max(pairs - 2, 0), pairs):
        for h in range(2):
            out_copy(2 * i + h, 2 * (i % 2) + h).wait()


def _stage_b(hyt, filt, short_w, short_b, skip):
    bsz, _, seq_len = hyt.shape
    cfg = _tiles(seq_len)
    chans, pitch, nb = cfg["ch_b"], cfg["slab_pitch"], cfg["nb"]
    parts = HYENA_ORDER + 1
    smem = pl.BlockSpec(memory_space=pltpu.SMEM)
    return pl.pallas_call(
        functools.partial(_stage_b_kernel, bsz=bsz, seq_len=seq_len, cfg=cfg),
        grid=(D_HYENA // chans,),
        in_specs=[smem, smem, smem,
                  pl.BlockSpec((HYENA_ORDER, chans * pitch, LANES), lambda g: (0, g, 0)),
                  pl.BlockSpec(memory_space=pl.ANY)],
        out_specs=pl.BlockSpec(memory_space=pl.ANY),
        out_shape=jax.ShapeDtypeStruct((bsz, D_HYENA, seq_len), F32),
        scratch_shapes=[
            pltpu.VMEM((4, parts, bsz, seq_len), F32),
            pltpu.VMEM((4, bsz, seq_len), F32),
            pltpu.VMEM((2, parts, bsz, seq_len), F32),
            pltpu.VMEM((HYENA_ORDER, TOEP, 2 * seq_len), BF16),
            pltpu.VMEM((HYENA_ORDER, TOEP, 2 * seq_len), BF16),
            pltpu.VMEM((2, nb * bsz, TOEP), BF16),
            pltpu.VMEM((2, nb * bsz, TOEP), F32),
            pltpu.VMEM((2, nb * bsz, TOEP), F32),
            pltpu.SemaphoreType.DMA((4, parts)),
            pltpu.SemaphoreType.DMA((4,)),
        ],
        compiler_params=_cparams(("arbitrary",)),
        name=f"stage_b_{seq_len}",
    )(short_w.reshape(-1), short_b.reshape(-1), skip.reshape(-1), filt, hyt)


def _stage_c_kernel(x_ref, gy_ref, sgb_ref, zbt_ref, mod_ref, whyo_ref, wout_ref, ln1g_ref, ln1b_ref,
                    w1_ref, b1_ref, w2_ref, b2_ref, ln2g_ref, ln2b_ref, o_ref, *, ff_chunk):
    gate1 = mod_ref[0, 2:3, :]
    shift2 = mod_ref[0, 3:4, :]
    scale2 = mod_ref[0, 4:5, :]
    gate2 = mod_ref[0, 5:6, :]
    zb = zbt_ref[0].T.astype(BF16)
    yb = jnp.dot(zb, whyo_ref[...], preferred_element_type=F32)
    m = gy_ref[0] + sgb_ref[0] * yb
    mix = jnp.dot(m.astype(BF16), wout_ref[...], preferred_element_type=F32)
    x1 = _ln_rows(DEEPNORM_ALPHA * x_ref[0] + gate1 * mix) * ln1g_ref[...] + ln1b_ref[...]
    h2 = (_ln_rows(x1) * (1.0 + scale2) + shift2).astype(BF16)
    f = jnp.zeros(x1.shape, F32) + b2_ref[...]
    for j in range(D_FF // ff_chunk):
        cols = slice(j * ff_chunk, (j + 1) * ff_chunk)
        u = jnp.maximum(jnp.dot(h2, w1_ref[:, cols], preferred_element_type=F32) + b1_ref[:, cols], 0.0)
        f = f + jnp.dot((u * u).astype(BF16), w2_ref[cols, :], preferred_element_type=F32)
    o_ref[0] = _ln_rows(DEEPNORM_ALPHA * x1 + gate2 * f) * ln2g_ref[...] + ln2b_ref[...]


def _stage_c(x, gy, sgb, zbt, mod, whyo, wout, ln1g, ln1b, w1, b1, w2, b2, ln2g, ln2b):
    bsz, seq_len, _ = x.shape
    tt = _tiles(seq_len)["tt_c"]
    tok = pl.BlockSpec((1, tt, D_MODEL), lambda b, i: (b, i, 0))
    return pl.pallas_call(
        functools.partial(_stage_c_kernel, ff_chunk=D_MODEL),
        grid=(bsz, seq_len // tt),
        in_specs=[tok, tok, tok,
                  pl.BlockSpec((1, D_HYENA, tt), lambda b, i: (b, 0, i)),
                  pl.BlockSpec((1, N_MOD, D_MODEL), lambda b, i: (b, 0, 0)),
                  _const_spec((D_HYENA, D_MODEL)), _const_spec((D_MODEL, D_MODEL)),
                  _const_spec((1, D_MODEL)), _const_spec((1, D_MODEL)),
                  _const_spec((D_MODEL, D_FF)), _const_spec((1, D_FF)),
                  _const_spec((D_FF, D_MODEL)), _const_spec((1, D_MODEL)),
                  _const_spec((1, D_MODEL)), _const_spec((1, D_MODEL))],
        out_specs=tok,
        out_shape=jax.ShapeDtypeStruct((bsz, seq_len, D_MODEL), F32),
        compiler_params=_cparams(("parallel", "parallel")),
        name=f"stage_c_{seq_len}",
    )(x, gy, sgb, zbt, mod, whyo, wout, ln1g, ln1b, w1, b1, w2, b2, ln2g, ln2b)


def _encoder(x, mod, filt, p):
    gy, sgb, hyt = _stage_a(x, mod, p["wa"], p["whyt"], p["wg"], p["dww"], p["dwb"], p["lng"], p["lnb"], p["wpw"])
    zbt = _stage_b(hyt, filt, p["short_w"], p["short_b"], p["skip"])
    return _stage_c(x, gy, sgb, zbt, mod, p["whyo"], p["wout"], p["ln1g"], p["ln1b"],
                    p["w1"], p["b1"], p["w2"], p["b2"], p["ln2g"], p["ln2b"])


def kernel(x_prompt, x_sample, c_prompt, c_sample, w_ada, b_ada, w_in, conv_dw_w, conv_dw_b, conv_ln_g, conv_ln_b, conv_pw_w, hy_short_w, hy_short_b, hy_ffn_w1, hy_ffn_b1, hy_sin_freq, hy_ffn_w2, hy_ffn_b2, hy_ffn_w3, hy_skip, hy_out_w, w_out, ln1_g, ln1_b, mlp_w1, mlp_b1, mlp_w2, mlp_b2, ln2_g, ln2_b):
    l = 0
    row = lambda v: v.reshape(1, -1)
    s1 = 2 * D_CONV
    s2 = s1 + (HYENA_ORDER + 1) * D_HYENA
    w_in_l = w_in[l]
    p = dict(
        wa=w_in_l[:, :s1].astype(BF16),
        whyt=w_in_l[:, s1:s2].T.astype(BF16),
        wg=w_in_l[:, s2:].astype(BF16),
        dww=conv_dw_w[l], dwb=row(conv_dw_b[l]), lng=row(conv_ln_g[l]), lnb=row(conv_ln_b[l]),
        wpw=conv_pw_w[l].astype(BF16),
        short_w=hy_short_w[l], short_b=hy_short_b[l], skip=hy_skip[l],
        whyo=hy_out_w[l].astype(BF16), wout=w_out[l].astype(BF16),
        ln1g=row(ln1_g[l]), ln1b=row(ln1_b[l]),
        w1=mlp_w1[l].astype(BF16), b1=row(mlp_b1[l]), w2=mlp_w2[l].astype(BF16), b2=row(mlp_b2[l]),
        ln2g=row(ln2_g[l]), ln2b=row(ln2_b[l]),
    )
    nb_p = x_prompt.shape[0]
    mod = _mod(jnp.concatenate([c_prompt, c_sample], axis=0), w_ada[l], b_ada[l])
    mod = mod.reshape(-1, N_MOD, D_MODEL)
    outs = []
    for x, m in ((x_prompt, mod[:nb_p]), (x_sample, mod[nb_p:])):
        filt = _filters(x.shape[1], hy_ffn_w1[l], hy_ffn_b1[l], hy_sin_freq[l],
                        hy_ffn_w2[l], hy_ffn_b2[l], hy_ffn_w3[l])
        outs.append(_encoder(x, m, filt, p))
    return tuple(outs)
```

```python
import functools
import math

import jax
import jax.numpy as jnp
from jax import lax
from jax.experimental import pallas as pl
from jax.experimental.pallas import tpu as pltpu

F32 = jnp.float32
BF16 = jnp.bfloat16
HIGHEST = lax.Precision.HIGHEST

LANES = 128
SUBLANES = 8
MXU_DIM = 256
VMEM_LIMIT_BYTES = 56 * 1024 * 1024

D_MODEL = 1024
D_CONV = D_MODEL // 2
D_HYENA = D_MODEL // 2
CONV_KERNEL = 31
SHORT_KERNEL = 3
HYENA_ORDER = 2
N_BANDS = 16
FILTER_EMB = 1 + 2 * N_BANDS
FILTER_HIDDEN = 64
N_DIRS = 2
D_FF = 4 * D_MODEL
N_MOD = 6
LN_EPS = 1e-5
DEPTH = 1
MAX_DECAY = math.log(1e-2) / 0.3
MIN_DECAY = math.log(1e-2) / 1.5
DEEPNORM_ALPHA = (2.0 * DEPTH) ** 0.25

TOEP = MXU_DIM
CONV_SPAN = 2 * TOEP
NT_DIMS = (((1,), (1,)), ((), ()))


def _strip_geometry(span):
    width = 2 * span + TOEP
    slabs = width // LANES
    return dict(width=width, slabs=slabs, pitch=-(-slabs // SUBLANES) * SUBLANES, centre=span + TOEP - 1)


def _tiles(seq_len):
    return dict(
        tt_a=512,
        tt_c=512,
        ch_b=32,
        filt_rows=128,
        nb=seq_len // TOEP,
        long=_strip_geometry(seq_len),
        conv=_strip_geometry(CONV_SPAN),
    )


def _cparams(sem):
    return pltpu.CompilerParams(dimension_semantics=sem, vmem_limit_bytes=VMEM_LIMIT_BYTES)


def _ln_rows(x):
    mu = jnp.mean(x, axis=-1, keepdims=True)
    xc = x - mu
    var = jnp.mean(xc * xc, axis=-1, keepdims=True)
    return xc * lax.rsqrt(var + LN_EPS)


def _mod_kernel(c_ref, w_ref, b_ref, o_ref):
    c = c_ref[...]
    s = c * jax.nn.sigmoid(c)
    o_ref[...] = jnp.dot(s, w_ref[...], precision=HIGHEST, preferred_element_type=F32) + b_ref[...]


def _mod(c_all, w_ada, b_ada):
    rows = c_all.shape[0]
    n = w_ada.shape[1]
    bn = D_MODEL
    return pl.pallas_call(
        _mod_kernel,
        grid=(n // bn,),
        in_specs=[pl.BlockSpec((rows, D_MODEL), lambda j: (0, 0)),
                  pl.BlockSpec((D_MODEL, bn), lambda j: (0, j)),
                  pl.BlockSpec((1, bn), lambda j: (0, j))],
        out_specs=pl.BlockSpec((rows, bn), lambda j: (0, j)),
        out_shape=jax.ShapeDtypeStruct((rows, n), F32),
        compiler_params=_cparams(("parallel",)),
        name="adaln_mod",
    )(c_all, w_ada, b_ada.reshape(1, n))


def _bf16_bits(x):
    b = lax.bitcast_convert_type(x, jnp.uint32)
    return (b + jnp.uint32(0x7FFF) + ((b >> 16) & jnp.uint32(1))) >> 16


def _filter_kernel(w1t_ref, b1_ref, fr_ref, w2t_ref, b2_ref, w3t_ref, o_ref, hdn_ref, *, seq_len, cfg):
    geo, rows = cfg["long"], cfg["filt_rows"]
    width, slabs, pitch, centre = geo["width"], geo["slabs"], geo["pitch"], geo["centre"]
    m = lax.broadcasted_iota(jnp.int32, (1, width), 1)
    tau = jnp.abs(m - centre).astype(F32)
    t = tau * (1.0 / (seq_len - 1))

    @pl.when(pl.program_id(0) == 0)
    def _():
        w = tau * (2.0 * math.pi / seq_len)
        bidx = lax.broadcasted_iota(jnp.int32, (N_BANDS, 1), 0).astype(F32)
        bands = 1e-4 + bidx * ((N_BANDS - 1 - 1e-4) / (N_BANDS - 1))
        ang = bands * w
        z = jnp.concatenate([t, jnp.cos(ang), -jnp.sin(ang)], axis=0)
        fr = fr_ref[...]
        h1 = jnp.sin(fr * (jnp.dot(w1t_ref[...], z, precision=HIGHEST, preferred_element_type=F32) + b1_ref[...]))
        hdn_ref[...] = jnp.sin(fr * (jnp.dot(w2t_ref[...], h1, precision=HIGHEST, preferred_element_type=F32)
                                     + b2_ref[...]))

    c0 = pl.program_id(0) * rows
    cidx = (lax.broadcasted_iota(jnp.int32, (rows, 1), 0) + c0).astype(F32)
    step = (MAX_DECAY - MIN_DECAY) / (D_HYENA - 1)
    delta_f = jnp.abs(MIN_DECAY + cidx * step)
    delta_b = jnp.abs(MIN_DECAY + (D_HYENA - 1 - cidx) * step)
    fwd = m >= centre
    decay = jnp.exp(-t * jnp.where(fwd, delta_f, delta_b))
    valid = tau <= float(seq_len - 1)
    hdn = hdn_ref[...]
    for o in range(HYENA_ORDER):
        hf = jnp.dot(w3t_ref[o, 0], hdn, precision=HIGHEST, preferred_element_type=F32)
        hb = jnp.dot(w3t_ref[o, 1], hdn, precision=HIGHEST, preferred_element_type=F32)
        k = jnp.where(valid, jnp.where(fwd, hf, hb) * decay, 0.0)
        word = _bf16_bits(k) | (_bf16_bits(pltpu.roll(k, 1, 1)) << 16)
        for s in range(slabs):
            o_ref[o, pl.ds(s, rows, stride=pitch), :] = word[:, s * LANES:(s + 1) * LANES]
        for s in range(slabs, pitch):
            o_ref[o, pl.ds(s, rows, stride=pitch), :] = jnp.zeros((rows, LANES), jnp.uint32)


def _filters(seq_len, w1, b1, freq, w2, b2, w3):
    cfg = _tiles(seq_len)
    rows, pitch, width = cfg["filt_rows"], cfg["long"]["pitch"], cfg["long"]["width"]
    w3t = w3.T.reshape(HYENA_ORDER, N_DIRS, D_HYENA, FILTER_HIDDEN)
    col = lambda v: v.reshape(FILTER_HIDDEN, 1)
    full = lambda shape: pl.BlockSpec(shape, lambda i: (0,) * len(shape))
    return pl.pallas_call(
        functools.partial(_filter_kernel, seq_len=seq_len, cfg=cfg),
        grid=(D_HYENA // rows,),
        in_specs=[full((FILTER_HIDDEN, FILTER_EMB)), full((FILTER_HIDDEN, 1)), full((FILTER_HIDDEN, 1)),
                  full((FILTER_HIDDEN, FILTER_HIDDEN)), full((FILTER_HIDDEN, 1)),
                  pl.BlockSpec((HYENA_ORDER, N_DIRS, rows, FILTER_HIDDEN), lambda i: (0, 0, i, 0))],
        out_specs=pl.BlockSpec((HYENA_ORDER, rows * pitch, LANES), lambda i: (0, i, 0)),
        out_shape=jax.ShapeDtypeStruct((HYENA_ORDER, D_HYENA * pitch, LANES), jnp.uint32),
        scratch_shapes=[pltpu.VMEM((FILTER_HIDDEN, width), F32)],
        compiler_params=_cparams(("arbitrary",)),
        name=f"hyena_filters_{seq_len}",
    )(w1.T, col(b1), col(freq), w2.T, col(b2), w3t)


def _conv_filter_words(dww):
    geo = _tiles(CONV_SPAN)["conv"]
    pad = CONV_KERNEL // 2
    tap = pad - (jnp.arange(geo["width"]) - geo["centre"])
    k = jnp.where((tap >= 0) & (tap < CONV_KERNEL), dww.T[:, jnp.clip(tap, 0, CONV_KERNEL - 1)], 0.0)
    bits = lambda v: lax.bitcast_convert_type(v.astype(BF16), jnp.uint16).astype(jnp.uint32)
    word = bits(k) | (bits(jnp.roll(k, 1, axis=1)) << 16)
    word = word.reshape(D_CONV, geo["slabs"], LANES)
    word = jnp.pad(word, ((0, 0), (0, geo["pitch"] - geo["slabs"]), (0, 0)))
    return word.reshape(1, D_CONV * geo["pitch"], LANES)


def _stage_a_kernel(x_ref, mod_ref, wct_ref, wg_ref, sga_ref, sgb_ref, at_ref, hyt_ref):
    shift1 = mod_ref[0, 0:1, :]
    scale1 = mod_ref[0, 1:2, :]
    h = (_ln_rows(x_ref[0]) * (1.0 + scale1) + shift1).astype(BF16)

    def channel_major(r0, rows):
        return lax.dot_general(wct_ref[r0:r0 + rows, :], h, NT_DIMS, preferred_element_type=F32)

    at_ref[0] = channel_major(0, D_CONV) * jax.nn.sigmoid(channel_major(D_CONV, D_CONV))
    for r in range(HYENA_ORDER + 1):
        hyt_ref[0, r * D_HYENA:(r + 1) * D_HYENA, :] = channel_major(2 * D_CONV + r * D_HYENA, D_HYENA)
    for c in range(2):
        cols = slice(c * D_CONV, (c + 1) * D_CONV)
        sga_ref[0, :, cols] = jax.nn.sigmoid(jnp.dot(h, wg_ref[:, cols], preferred_element_type=F32))
        colsb = slice(D_MODEL + c * D_CONV, D_MODEL + (c + 1) * D_CONV)
        sgb_ref[0, :, cols] = jax.nn.sigmoid(jnp.dot(h, wg_ref[:, colsb], preferred_element_type=F32))


def _const_spec(shape):
    return pl.BlockSpec(shape, lambda b, i: (0,) * len(shape), pipeline_mode=pl.Buffered(1))


def _stage_a(x, mod, wct, wg):
    bsz, seq_len, _ = x.shape
    tt = _tiles(seq_len)["tt_a"]
    hy_rows = (HYENA_ORDER + 1) * D_HYENA
    tok = pl.BlockSpec((1, tt, D_MODEL), lambda b, i: (b, i, 0))
    return pl.pallas_call(
        _stage_a_kernel,
        grid=(bsz, seq_len // tt),
        in_specs=[tok,
                  pl.BlockSpec((1, N_MOD, D_MODEL), lambda b, i: (b, 0, 0)),
                  _const_spec((2 * D_CONV + hy_rows, D_MODEL)),
                  _const_spec((D_MODEL, 2 * D_MODEL))],
        out_specs=[tok, tok,
                   pl.BlockSpec((1, D_CONV, tt), lambda b, i: (b, 0, i)),
                   pl.BlockSpec((1, hy_rows, tt), lambda b, i: (b, 0, i))],
        out_shape=[jax.ShapeDtypeStruct((bsz, seq_len, D_MODEL), F32),
                   jax.ShapeDtypeStruct((bsz, seq_len, D_MODEL), F32),
                   jax.ShapeDtypeStruct((bsz, D_CONV, seq_len), F32),
                   jax.ShapeDtypeStruct((bsz, hy_rows, seq_len), F32)],
        compiler_params=_cparams(("parallel", "parallel")),
        name=f"stage_a_{seq_len}",
    )(x, mod, wct, wg)


def _stage_b_kernel(sw_ref, sb_ref, skip_ref, dwb_ref, filt_ref, cfilt_ref, at_hbm, hyt_hbm, zb_hbm, cv_hbm,
                    inbuf, outbuf, conv_ref, toep_a, toep_b, ctoep_a, ctoep_b, lhs_ref, clhs_ref, zf_ref, acc_ref,
                    in_sem, out_sem, *, bsz, seq_len, cfg):
    nb, chans = cfg["nb"], cfg["ch_b"]
    geo, cgeo = cfg["long"], cfg["conv"]
    hy_parts = HYENA_ORDER + 1
    parts = hy_parts + 1
    pairs = chans // 2
    c_base = pl.program_id(0) * chans

    def in_copy(j, slot, part):
        c = c_base + j
        src = at_hbm.at[:, c, :] if part == hy_parts else hyt_hbm.at[:, part * D_HYENA + c, :]
        return pltpu.make_async_copy(src, inbuf.at[slot, part], in_sem.at[slot, part])

    def out_copy(j, slot, kind):
        dst = (zb_hbm, cv_hbm)[kind].at[:, c_base + j, :]
        return pltpu.make_async_copy(outbuf.at[slot, kind], dst, out_sem.at[slot, kind])

    lane = lax.broadcasted_iota(jnp.int32, (bsz, seq_len), 1)
    half = LANES // 2
    word_row = lax.broadcasted_iota(jnp.int32, (half, LANES), 0)
    col = lax.broadcasted_iota(jnp.int32, (half, LANES), 1)
    low = col <= 2 * word_row

    def build_strip(dst_ref, src_ref, row0, k_lo, k_hi, n_tiles):
        def put(offset, row_half, big):
            q, within = divmod(offset, TOEP)
            if 1 <= q <= n_tiles:
                r0 = (q - 1) * TOEP + row_half * LANES
                dst_ref[r0:r0 + LANES, within:within + LANES] = big

        prev = None
        for k in range(k_lo, k_hi + 1):
            fb = jnp.broadcast_to(src_ref[pl.ds(row0 + k, 1), :], (half, LANES))
            cur = pltpu.roll(fb, 1, 1, stride=2, stride_axis=0)
            if prev is not None:
                big = pltpu.bitcast(jnp.where(low, prev, cur), BF16)
                put((k - 1) * LANES, 1, big)
                if k >= 2:
                    put((k - 2) * LANES, 0, big)
            prev = cur

    ck_lo = (cgeo["centre"] - CONV_KERNEL // 2) // LANES - 1
    ck_hi = (cgeo["centre"] + CONV_KERNEL // 2) // LANES + 1
    long_tiles = 2 * nb - 1
    conv_tiles = 2 * (CONV_SPAN // TOEP) - 1

    def build_strips(toep_ref, ctoep_ref, j):
        for o in range(HYENA_ORDER):
            build_strip(toep_ref.at[o], filt_ref.at[o], j * geo["pitch"], 0, geo["slabs"] - 1, long_tiles)
        build_strip(ctoep_ref, cfilt_ref.at[0], j * cgeo["pitch"], ck_lo, ck_hi, conv_tiles)

    def spread(dst_ref, h, n_tiles, block):
        reach = n_tiles // 2
        for si in range(nb):
            blk = block(si).astype(BF16)
            for q in range(n_tiles):
                tj = si + q - reach
                if 0 <= tj < nb:
                    dst_ref[h, tj * bsz:(tj + 1) * bsz, q * TOEP:(q + 1) * TOEP] = blk

    def apply(src_ref, h, stack):
        k = src_ref.shape[-1]
        ks = -(-(k // TOEP) // 2) * TOEP
        res = jnp.dot(src_ref[h, :, :ks], stack[:ks, :], preferred_element_type=F32)
        if ks < k:
            res = res + jnp.dot(src_ref[h, :, ks:], stack[ks:, :], preferred_element_type=F32)
        return res

    def long_conv(toep_ref, o, h):
        def tile(d):
            return toep_ref[o, (nb - 1 + d) * TOEP:(nb + d) * TOEP, :]
        acc_ref[h] = jnp.dot(lhs_ref[h], tile(0), preferred_element_type=F32)
        for d in range(1, nb):
            keep = (nb - d) * bsz
            acc_ref[h, d * bsz:, :] += jnp.dot(lhs_ref[h, :keep, :], tile(d), preferred_element_type=F32)
            acc_ref[h, :keep, :] += jnp.dot(lhs_ref[h, d * bsz:, :], tile(-d), preferred_element_type=F32)

    def col(si):
        return slice(si * TOEP, (si + 1) * TOEP)

    def row(si):
        return slice(si * bsz, (si + 1) * bsz)

    def channel(j, h, slot, toep_ref, ctoep_ref):
        c = c_base + j
        spread(clhs_ref, h, conv_tiles, lambda si: inbuf[slot, hy_parts, :, col(si)])
        cv = apply(clhs_ref, h, ctoep_ref)
        for si in range(nb):
            outbuf[slot, 1, :, col(si)] = cv[row(si), :] + dwb_ref[c]
        out_copy(j, slot, 1).start()

        for part in range(hy_parts):
            p = inbuf[slot, part]
            ch = part * D_HYENA + c
            before = jnp.where(lane == 0, 0.0, pltpu.roll(p, 1, 1))
            after = jnp.where(lane == seq_len - 1, 0.0, pltpu.roll(p, seq_len - 1, 1))
            conv_ref[h, part] = (sw_ref[ch] * before + sw_ref[hy_parts * D_HYENA + ch] * p
                                 + sw_ref[2 * hy_parts * D_HYENA + ch] * after + sb_ref[ch])
        for si in range(nb):
            lhs_ref[h, row(si), :] = conv_ref[h, 0, :, col(si)].astype(BF16)
        long_conv(toep_ref, 0, h)
        skip0 = skip_ref[c]
        for si in range(nb):
            z = conv_ref[h, 1, :, col(si)] * (acc_ref[h, row(si), :] + conv_ref[h, 0, :, col(si)] * skip0)
            zf_ref[h, row(si), :] = z
            lhs_ref[h, row(si), :] = z.astype(BF16)
        long_conv(toep_ref, 1, h)
        skip1 = skip_ref[D_HYENA + c]
        for si in range(nb):
            outbuf[slot, 0, :, col(si)] = (conv_ref[h, 2, :, col(si)]
                                           * (acc_ref[h, row(si), :] + zf_ref[h, row(si), :] * skip1))
        out_copy(j, slot, 0).start()

    def pair(i, carry):
        j0 = 2 * i
        s0 = 2 * (i % 2)
        for h in range(2):
            for part in range(parts):
                in_copy(j0 + h, s0 + h, part).wait()

        @pl.when(i + 1 < pairs)
        def _():
            for h in range(2):
                for part in range(parts):
                    in_copy(j0 + 2 + h, 2 - s0 + h, part).start()

        @pl.when(i >= 2)
        def _():
            for h in range(2):
                for kind in range(2):
                    out_copy(j0 - 4 + h, s0 + h, kind).wait()

        build_strips(toep_b, ctoep_b, j0 + 1)
        channel(j0, 0, s0, toep_a, ctoep_a)
        build_strips(toep_a, ctoep_a, jnp.minimum(j0 + 2, chans - 1))
        channel(j0 + 1, 1, s0 + 1, toep_b, ctoep_b)
        return carry

    for h in range(2):
        for part in range(parts):
            in_copy(h, h, part).start()
    for ref in (ctoep_a, ctoep_b, clhs_ref):
        ref[...] = jnp.zeros(ref.shape, BF16)
    build_strips(toep_a, ctoep_a, 0)
    lax.fori_loop(0, pairs, pair, 0)
    for i in range(max(pairs - 2, 0), pairs):
        for h in range(2):
            for kind in range(2):
                out_copy(2 * i + h, 2 * (i % 2) + h, kind).wait()


def _stage_b(at, hyt, filt, cfilt, short_w, short_b, skip, dwb):
    bsz, _, seq_len = hyt.shape
    cfg = _tiles(seq_len)
    chans, nb = cfg["ch_b"], cfg["nb"]
    parts = HYENA_ORDER + 2
    long_k = (2 * nb - 1) * TOEP
    conv_k = (2 * (CONV_SPAN // TOEP) - 1) * TOEP
    smem = pl.BlockSpec(memory_space=pltpu.SMEM)
    hbm = pl.BlockSpec(memory_space=pl.ANY)
    chan_major = jax.ShapeDtypeStruct((bsz, D_HYENA, seq_len), F32)
    return pl.pallas_call(
        functools.partial(_stage_b_kernel, bsz=bsz, seq_len=seq_len, cfg=cfg),
        grid=(D_HYENA // chans,),
        in_specs=[smem, smem, smem, smem,
                  pl.BlockSpec((HYENA_ORDER, chans * cfg["long"]["pitch"], LANES), lambda g: (0, g, 0)),
                  pl.BlockSpec((1, chans * cfg["conv"]["pitch"], LANES), lambda g: (0, g, 0)),
                  hbm, hbm],
        out_specs=[hbm, hbm],
        out_shape=[chan_major, chan_major],
        scratch_shapes=[
            pltpu.VMEM((4, parts, bsz, seq_len), F32),
            pltpu.VMEM((4, 2, bsz, seq_len), F32),
            pltpu.VMEM((2, HYENA_ORDER + 1, bsz, seq_len), F32),
            pltpu.VMEM((HYENA_ORDER, long_k, TOEP), BF16),
            pltpu.VMEM((HYENA_ORDER, long_k, TOEP), BF16),
            pltpu.VMEM((conv_k, TOEP), BF16),
            pltpu.VMEM((conv_k, TOEP), BF16),
            pltpu.VMEM((2, nb * bsz, TOEP), BF16),
            pltpu.VMEM((2, nb * bsz, conv_k), BF16),
            pltpu.VMEM((2, nb * bsz, TOEP), F32),
            pltpu.VMEM((2, nb * bsz, TOEP), F32),
            pltpu.SemaphoreType.DMA((4, parts)),
            pltpu.SemaphoreType.DMA((4, 2)),
        ],
        compiler_params=_cparams(("arbitrary",)),
        name=f"stage_b_{seq_len}",
    )(short_w.reshape(-1), short_b.reshape(-1), skip.reshape(-1), dwb.reshape(-1), filt, cfilt, at, hyt)


def _stage_c_kernel(x_ref, sga_ref, sgb_ref, zbt_ref, cvt_ref, mod_ref, lng_ref, lnb_ref, wpw_ref, whyo_ref,
                    wout_ref, ln1g_ref, ln1b_ref, w1_ref, b1_ref, w2_ref, b2_ref, ln2g_ref, ln2b_ref, o_ref,
                    *, ff_chunk):
    gate1 = mod_ref[0, 2:3, :]
    shift2 = mod_ref[0, 3:4, :]
    scale2 = mod_ref[0, 4:5, :]
    gate2 = mod_ref[0, 5:6, :]
    half = x_ref.shape[1] // 2

    def mixer(rows):
        a2 = _ln_rows(cvt_ref[0, :, rows].T) * lng_ref[...] + lnb_ref[...]
        a2 = (a2 * jax.nn.sigmoid(a2)).astype(BF16)
        ya = jnp.dot(a2, wpw_ref[...], preferred_element_type=F32)
        zb = zbt_ref[0, :, rows].T.astype(BF16)
        yb = jnp.dot(zb, whyo_ref[...], preferred_element_type=F32)
        m = sga_ref[0, rows, :] * ya + sgb_ref[0, rows, :] * yb
        mix = jnp.dot(m.astype(BF16), wout_ref[...], preferred_element_type=F32)
        x1 = _ln_rows(DEEPNORM_ALPHA * x_ref[0, rows, :] + gate1 * mix) * ln1g_ref[...] + ln1b_ref[...]
        return x1, (_ln_rows(x1) * (1.0 + scale2) + shift2).astype(BF16)

    def mlp(x1, h2, rows):
        f = jnp.zeros(x1.shape, F32) + b2_ref[...]
        for j in range(D_FF // ff_chunk):
            cols = slice(j * ff_chunk, (j + 1) * ff_chunk)
            u = jnp.maximum(jnp.dot(h2, w1_ref[:, cols], preferred_element_type=F32) + b1_ref[:, cols], 0.0)
            f = f + jnp.dot((u * u).astype(BF16), w2_ref[cols, :], preferred_element_type=F32)
        o_ref[0, rows, :] = _ln_rows(DEEPNORM_ALPHA * x1 + gate2 * f) * ln2g_ref[...] + ln2b_ref[...]

    rows_a, rows_b = slice(0, half), slice(half, 2 * half)
    x1_a, h2_a = mixer(rows_a)
    x1_b, h2_b = mixer(rows_b)
    mlp(x1_a, h2_a, rows_a)
    mlp(x1_b, h2_b, rows_b)


def _stage_c(x, sga, sgb, zbt, cvt, mod, lng, lnb, wpw, whyo, wout, ln1g, ln1b, w1, b1, w2, b2, ln2g, ln2b):
    bsz, seq_len, _ = x.shape
    tt = _tiles(seq_len)["tt_c"]
    tok = pl.BlockSpec((1, tt, D_MODEL), lambda b, i: (b, i, 0))
    chan = pl.BlockSpec((1, D_HYENA, tt), lambda b, i: (b, 0, i))
    return pl.pallas_call(
        functools.partial(_stage_c_kernel, ff_chunk=D_MODEL),
        grid=(bsz, seq_len // tt),
        in_specs=[tok, tok, tok, chan, chan,
                  pl.BlockSpec((1, N_MOD, D_MODEL), lambda b, i: (b, 0, 0)),
                  _const_spec((1, D_CONV)), _const_spec((1, D_CONV)), _const_spec((D_CONV, D_MODEL)),
                  _const_spec((D_HYENA, D_MODEL)), _const_spec((D_MODEL, D_MODEL)),
                  _const_spec((1, D_MODEL)), _const_spec((1, D_MODEL)),
                  _const_spec((D_MODEL, D_FF)), _const_spec((1, D_FF)),
                  _const_spec((D_FF, D_MODEL)), _const_spec((1, D_MODEL)),
                  _const_spec((1, D_MODEL)), _const_spec((1, D_MODEL))],
        out_specs=tok,
        out_shape=jax.ShapeDtypeStruct((bsz, seq_len, D_MODEL), F32),
        compiler_params=_cparams(("parallel", "parallel")),
        name=f"stage_c_{seq_len}",
    )(x, sga, sgb, zbt, cvt, mod, lng, lnb, wpw, whyo, wout, ln1g, ln1b, w1, b1, w2, b2, ln2g, ln2b)


def _encoder(x, mod, filt, p):
    sga, sgb, at, hyt = _stage_a(x, mod, p["wct"], p["wg"])
    zbt, cvt = _stage_b(at, hyt, filt, p["cfilt"], p["short_w"], p["short_b"], p["skip"], p["dwb"])
    return _stage_c(x, sga, sgb, zbt, cvt, mod, p["lng"], p["lnb"], p["wpw"], p["whyo"], p["wout"],
                    p["ln1g"], p["ln1b"], p["w1"], p["b1"], p["w2"], p["b2"], p["ln2g"], p["ln2b"])


def _prepare(w_in, conv_dw_w, conv_dw_b, conv_ln_g, conv_ln_b, conv_pw_w, hy_short_w, hy_short_b, hy_skip,
             hy_out_w, w_out, ln1_g, ln1_b, mlp_w1, mlp_b1, mlp_w2, mlp_b2, ln2_g, ln2_b, l=0):
    row = lambda v: v.reshape(1, -1)
    s2 = 2 * D_CONV + (HYENA_ORDER + 1) * D_HYENA
    w_in_l = w_in[l]
    return dict(
        wct=w_in_l[:, :s2].T.astype(BF16),
        wg=w_in_l[:, s2:].astype(BF16),
        cfilt=_conv_filter_words(conv_dw_w[l]), dwb=conv_dw_b[l],
        lng=row(conv_ln_g[l]), lnb=row(conv_ln_b[l]), wpw=conv_pw_w[l].astype(BF16),
        short_w=hy_short_w[l], short_b=hy_short_b[l], skip=hy_skip[l],
        whyo=hy_out_w[l].astype(BF16), wout=w_out[l].astype(BF16),
        ln1g=row(ln1_g[l]), ln1b=row(ln1_b[l]),
        w1=mlp_w1[l].astype(BF16), b1=row(mlp_b1[l]), w2=mlp_w2[l].astype(BF16), b2=row(mlp_b2[l]),
        ln2g=row(ln2_g[l]), ln2b=row(ln2_b[l]),
    )


def kernel(x_prompt, x_sample, c_prompt, c_sample, w_ada, b_ada, w_in, conv_dw_w, conv_dw_b, conv_ln_g, conv_ln_b, conv_pw_w, hy_short_w, hy_short_b, hy_ffn_w1, hy_ffn_b1, hy_sin_freq, hy_ffn_w2, hy_ffn_b2, hy_ffn_w3, hy_skip, hy_out_w, w_out, ln1_g, ln1_b, mlp_w1, mlp_b1, mlp_w2, mlp_b2, ln2_g, ln2_b):
    l = 0
    p = _prepare(w_in, conv_dw_w, conv_dw_b, conv_ln_g, conv_ln_b, conv_pw_w, hy_short_w, hy_short_b, hy_skip,
                 hy_out_w, w_out, ln1_g, ln1_b, mlp_w1, mlp_b1, mlp_w2, mlp_b2, ln2_g, ln2_b, l)
    nb_p = x_prompt.shape[0]
    mod = _mod(jnp.concatenate([c_prompt, c_sample], axis=0), w_ada[l], b_ada[l])
    mod = mod.reshape(-1, N_MOD, D_MODEL)
    outs = []
    for x, m in ((x_prompt, mod[:nb_p]), (x_sample, mod[nb_p:])):
        filt = _filters(x.shape[1], hy_ffn_w1[l], hy_ffn_b1[l], hy_sin_freq[l],
                        hy_ffn_w2[l], hy_ffn_b2[l], hy_ffn_w3[l])
        outs.append(_encoder(x, m, filt, p))
    return tuple(outs)
```

```python
import functools
import math

import jax
import jax.numpy as jnp
from jax import lax
from jax.experimental import pallas as pl
from jax.experimental.pallas import tpu as pltpu

F32 = jnp.float32
BF16 = jnp.bfloat16
HIGHEST = lax.Precision.HIGHEST

LANES = 128
SUBLANES = 8
MXU_DIM = 256
VMEM_LIMIT_BYTES = 56 * 1024 * 1024

D_MODEL = 1024
D_CONV = D_MODEL // 2
D_HYENA = D_MODEL // 2
CONV_KERNEL = 31
SHORT_KERNEL = 3
HYENA_ORDER = 2
N_BANDS = 16
FILTER_EMB = 1 + 2 * N_BANDS
FILTER_HIDDEN = 64
N_DIRS = 2
D_FF = 4 * D_MODEL
N_MOD = 6
LN_EPS = 1e-5
DEPTH = 1
MAX_DECAY = math.log(1e-2) / 0.3
MIN_DECAY = math.log(1e-2) / 1.5
DEEPNORM_ALPHA = (2.0 * DEPTH) ** 0.25

TOEP = MXU_DIM
CONV_SPAN = 2 * TOEP
NT_DIMS = (((1,), (1,)), ((), ()))


def _strip_geometry(span):
    width = 2 * span + TOEP
    slabs = width // LANES
    return dict(width=width, slabs=slabs, pitch=-(-slabs // SUBLANES) * SUBLANES, centre=span + TOEP - 1)


def _tiles(seq_len):
    return dict(
        tt_a=512,
        tt_c=512,
        ch_b=32,
        filt_rows=128,
        nb=seq_len // TOEP,
        long=_strip_geometry(seq_len),
        conv=_strip_geometry(CONV_SPAN),
    )


def _cparams(sem):
    return pltpu.CompilerParams(dimension_semantics=sem, vmem_limit_bytes=VMEM_LIMIT_BYTES)


def _ln_rows(x):
    mu = jnp.mean(x, axis=-1, keepdims=True)
    xc = x - mu
    var = jnp.mean(xc * xc, axis=-1, keepdims=True)
    return xc * lax.rsqrt(var + LN_EPS)


def _mod_kernel(c_ref, w_ref, b_ref, o_ref):
    c = c_ref[...]
    s = c * jax.nn.sigmoid(c)
    o_ref[...] = jnp.dot(s, w_ref[...], precision=HIGHEST, preferred_element_type=F32) + b_ref[...]


def _mod(c_all, w_ada, b_ada):
    rows = c_all.shape[0]
    n = w_ada.shape[1]
    bn = D_MODEL
    return pl.pallas_call(
        _mod_kernel,
        grid=(n // bn,),
        in_specs=[pl.BlockSpec((rows, D_MODEL), lambda j: (0, 0)),
                  pl.BlockSpec((D_MODEL, bn), lambda j: (0, j)),
                  pl.BlockSpec((1, bn), lambda j: (0, j))],
        out_specs=pl.BlockSpec((rows, bn), lambda j: (0, j)),
        out_shape=jax.ShapeDtypeStruct((rows, n), F32),
        compiler_params=_cparams(("parallel",)),
        name="adaln_mod",
    )(c_all, w_ada, b_ada.reshape(1, n))


def _bf16_bits(x):
    b = lax.bitcast_convert_type(x, jnp.uint32)
    return (b + jnp.uint32(0x7FFF) + ((b >> 16) & jnp.uint32(1))) >> 16


def _filter_kernel(w1t_ref, b1_ref, fr_ref, w2t_ref, b2_ref, w3t_ref, o_ref, hdn_ref, *, seq_len, cfg):
    geo, rows = cfg["long"], cfg["filt_rows"]
    width, slabs, pitch, centre = geo["width"], geo["slabs"], geo["pitch"], geo["centre"]
    m = lax.broadcasted_iota(jnp.int32, (1, width), 1)
    tau = jnp.abs(m - centre).astype(F32)
    t = tau * (1.0 / (seq_len - 1))

    @pl.when(pl.program_id(0) == 0)
    def _():
        w = tau * (2.0 * math.pi / seq_len)
        bidx = lax.broadcasted_iota(jnp.int32, (N_BANDS, 1), 0).astype(F32)
        bands = 1e-4 + bidx * ((N_BANDS - 1 - 1e-4) / (N_BANDS - 1))
        ang = bands * w
        z = jnp.concatenate([t, jnp.cos(ang), -jnp.sin(ang)], axis=0)
        fr = fr_ref[...]
        h1 = jnp.sin(fr * (jnp.dot(w1t_ref[...], z, precision=HIGHEST, preferred_element_type=F32) + b1_ref[...]))
        hdn_ref[...] = jnp.sin(fr * (jnp.dot(w2t_ref[...], h1, precision=HIGHEST, preferred_element_type=F32)
                                     + b2_ref[...]))

    c0 = pl.program_id(0) * rows
    cidx = (lax.broadcasted_iota(jnp.int32, (rows, 1), 0) + c0).astype(F32)
    step = (MAX_DECAY - MIN_DECAY) / (D_HYENA - 1)
    delta_f = jnp.abs(MIN_DECAY + cidx * step)
    delta_b = jnp.abs(MIN_DECAY + (D_HYENA - 1 - cidx) * step)
    fwd = m >= centre
    decay = jnp.exp(-t * jnp.where(fwd, delta_f, delta_b))
    valid = tau <= float(seq_len - 1)
    hdn = hdn_ref[...]
    for o in range(HYENA_ORDER):
        hf = jnp.dot(w3t_ref[o, 0], hdn, precision=HIGHEST, preferred_element_type=F32)
        hb = jnp.dot(w3t_ref[o, 1], hdn, precision=HIGHEST, preferred_element_type=F32)
        k = jnp.where(valid, jnp.where(fwd, hf, hb) * decay, 0.0)
        word = _bf16_bits(k) | (_bf16_bits(pltpu.roll(k, 1, 1)) << 16)
        for s in range(slabs):
            o_ref[o, pl.ds(s, rows, stride=pitch), :] = word[:, s * LANES:(s + 1) * LANES]
        for s in range(slabs, pitch):
            o_ref[o, pl.ds(s, rows, stride=pitch), :] = jnp.zeros((rows, LANES), jnp.uint32)


def _filters(seq_len, w1, b1, freq, w2, b2, w3):
    cfg = _tiles(seq_len)
    rows, pitch, width = cfg["filt_rows"], cfg["long"]["pitch"], cfg["long"]["width"]
    w3t = w3.T.reshape(HYENA_ORDER, N_DIRS, D_HYENA, FILTER_HIDDEN)
    col = lambda v: v.reshape(FILTER_HIDDEN, 1)
    full = lambda shape: pl.BlockSpec(shape, lambda i: (0,) * len(shape))
    return pl.pallas_call(
        functools.partial(_filter_kernel, seq_len=seq_len, cfg=cfg),
        grid=(D_HYENA // rows,),
        in_specs=[full((FILTER_HIDDEN, FILTER_EMB)), full((FILTER_HIDDEN, 1)), full((FILTER_HIDDEN, 1)),
                  full((FILTER_HIDDEN, FILTER_HIDDEN)), full((FILTER_HIDDEN, 1)),
                  pl.BlockSpec((HYENA_ORDER, N_DIRS, rows, FILTER_HIDDEN), lambda i: (0, 0, i, 0))],
        out_specs=pl.BlockSpec((HYENA_ORDER, rows * pitch, LANES), lambda i: (0, i, 0)),
        out_shape=jax.ShapeDtypeStruct((HYENA_ORDER, D_HYENA * pitch, LANES), jnp.uint32),
        scratch_shapes=[pltpu.VMEM((FILTER_HIDDEN, width), F32)],
        compiler_params=_cparams(("arbitrary",)),
        name=f"hyena_filters_{seq_len}",
    )(w1.T, col(b1), col(freq), w2.T, col(b2), w3t)


def _conv_filter_words(dww):
    geo = _tiles(CONV_SPAN)["conv"]
    pad = CONV_KERNEL // 2
    tap = pad - (jnp.arange(geo["width"]) - geo["centre"])
    k = jnp.where((tap >= 0) & (tap < CONV_KERNEL), dww.T[:, jnp.clip(tap, 0, CONV_KERNEL - 1)], 0.0)
    bits = lambda v: lax.bitcast_convert_type(v.astype(BF16), jnp.uint16).astype(jnp.uint32)
    word = bits(k) | (bits(jnp.roll(k, 1, axis=1)) << 16)
    word = word.reshape(D_CONV, geo["slabs"], LANES)
    word = jnp.pad(word, ((0, 0), (0, geo["pitch"] - geo["slabs"]), (0, 0)))
    return word.reshape(1, D_CONV * geo["pitch"], LANES)


def _stage_a_kernel(x_ref, mod_ref, wct_ref, wg_ref, sga_ref, sgb_ref, at_ref, hyt_ref):
    shift1 = mod_ref[0, 0:1, :]
    scale1 = mod_ref[0, 1:2, :]
    h = (_ln_rows(x_ref[0]) * (1.0 + scale1) + shift1).astype(BF16)

    def channel_major(r0, rows):
        return lax.dot_general(wct_ref[r0:r0 + rows, :], h, NT_DIMS, preferred_element_type=F32)

    at_ref[0] = channel_major(0, D_CONV) * jax.nn.sigmoid(channel_major(D_CONV, D_CONV))
    for r in range(HYENA_ORDER + 1):
        hyt_ref[0, r * D_HYENA:(r + 1) * D_HYENA, :] = channel_major(2 * D_CONV + r * D_HYENA, D_HYENA)
    for c in range(2):
        cols = slice(c * D_CONV, (c + 1) * D_CONV)
        sga_ref[0, :, cols] = jax.nn.sigmoid(jnp.dot(h, wg_ref[:, cols], preferred_element_type=F32))
        colsb = slice(D_MODEL + c * D_CONV, D_MODEL + (c + 1) * D_CONV)
        sgb_ref[0, :, cols] = jax.nn.sigmoid(jnp.dot(h, wg_ref[:, colsb], preferred_element_type=F32))


def _const_spec(shape):
    return pl.BlockSpec(shape, lambda b, i: (0,) * len(shape), pipeline_mode=pl.Buffered(1))


def _stage_a(x, mod, wct, wg):
    bsz, seq_len, _ = x.shape
    tt = _tiles(seq_len)["tt_a"]
    hy_rows = (HYENA_ORDER + 1) * D_HYENA
    tok = pl.BlockSpec((1, tt, D_MODEL), lambda b, i: (b, i, 0))
    return pl.pallas_call(
        _stage_a_kernel,
        grid=(bsz, seq_len // tt),
        in_specs=[tok,
                  pl.BlockSpec((1, N_MOD, D_MODEL), lambda b, i: (b, 0, 0)),
                  _const_spec((2 * D_CONV + hy_rows, D_MODEL)),
                  _const_spec((D_MODEL, 2 * D_MODEL))],
        out_specs=[tok, tok,
                   pl.BlockSpec((1, D_CONV, tt), lambda b, i: (b, 0, i)),
                   pl.BlockSpec((1, hy_rows, tt), lambda b, i: (b, 0, i))],
        out_shape=[jax.ShapeDtypeStruct((bsz, seq_len, D_MODEL), F32),
                   jax.ShapeDtypeStruct((bsz, seq_len, D_MODEL), F32),
                   jax.ShapeDtypeStruct((bsz, D_CONV, seq_len), F32),
                   jax.ShapeDtypeStruct((bsz, hy_rows, seq_len), F32)],
        compiler_params=_cparams(("parallel", "parallel")),
        name=f"stage_a_{seq_len}",
    )(x, mod, wct, wg)


def _stage_b_kernel(sw_ref, sb_ref, skip_ref, dwb_ref, filt_ref, cfilt_ref, at_hbm, hyt_hbm, zb_hbm, cv_hbm,
                    inbuf, outbuf, conv_ref, toep_a, toep_b, toep_c, toep_d, ctoep_a, ctoep_b, ctoep_c, ctoep_d,
                    lhs_ref, clhs_ref, zf_ref, acc_ref, in_sem, out_sem, *, bsz, seq_len, cfg):
    nb, chans = cfg["nb"], cfg["ch_b"]
    geo, cgeo = cfg["long"], cfg["conv"]
    hy_parts = HYENA_ORDER + 1
    parts = hy_parts + 1
    pairs = chans // 2
    c_base = pl.program_id(0) * chans

    def in_copy(j, slot, part):
        c = c_base + j
        src = at_hbm.at[:, c, :] if part == hy_parts else hyt_hbm.at[:, part * D_HYENA + c, :]
        return pltpu.make_async_copy(src, inbuf.at[slot, part], in_sem.at[slot, part])

    def out_copy(j, slot, kind):
        dst = (zb_hbm, cv_hbm)[kind].at[:, c_base + j, :]
        return pltpu.make_async_copy(outbuf.at[slot, kind], dst, out_sem.at[slot, kind])

    lane = lax.broadcasted_iota(jnp.int32, (bsz, seq_len), 1)
    half = LANES // 2
    word_row = lax.broadcasted_iota(jnp.int32, (half, LANES), 0)
    col = lax.broadcasted_iota(jnp.int32, (half, LANES), 1)
    low = col <= 2 * word_row

    def build_strip(dst_ref, src_ref, row0, k_lo, k_hi, n_tiles):
        def put(offset, row_half, big):
            q, within = divmod(offset, TOEP)
            if 1 <= q <= n_tiles:
                r0 = (q - 1) * TOEP + row_half * LANES
                dst_ref[r0:r0 + LANES, within:within + LANES] = big

        prev = None
        for k in range(k_lo, k_hi + 1):
            fb = jnp.broadcast_to(src_ref[pl.ds(row0 + k, 1), :], (half, LANES))
            cur = pltpu.roll(fb, 1, 1, stride=2, stride_axis=0)
            if prev is not None:
                big = pltpu.bitcast(jnp.where(low, prev, cur), BF16)
                put((k - 1) * LANES, 1, big)
                if k >= 2:
                    put((k - 2) * LANES, 0, big)
            prev = cur

    ck_lo = (cgeo["centre"] - CONV_KERNEL // 2) // LANES - 1
    ck_hi = (cgeo["centre"] + CONV_KERNEL // 2) // LANES + 1
    long_tiles = 2 * nb - 1
    conv_tiles = 2 * (CONV_SPAN // TOEP) - 1

    def build_strips(toep_ref, ctoep_ref, j):
        for o in range(HYENA_ORDER):
            build_strip(toep_ref.at[o], filt_ref.at[o], j * geo["pitch"], 0, geo["slabs"] - 1, long_tiles)
        build_strip(ctoep_ref, cfilt_ref.at[0], j * cgeo["pitch"], ck_lo, ck_hi, conv_tiles)

    def spread(dst_ref, h, n_tiles, block):
        reach = n_tiles // 2
        for si in range(nb):
            blk = block(si).astype(BF16)
            for q in range(n_tiles):
                tj = si + q - reach
                if 0 <= tj < nb:
                    dst_ref[h, tj * bsz:(tj + 1) * bsz, q * TOEP:(q + 1) * TOEP] = blk

    def apply(src_ref, h, stack):
        k = src_ref.shape[-1]
        ks = -(-(k // TOEP) // 2) * TOEP
        res = jnp.dot(src_ref[h, :, :ks], stack[:ks, :], preferred_element_type=F32)
        if ks < k:
            res = res + jnp.dot(src_ref[h, :, ks:], stack[ks:, :], preferred_element_type=F32)
        return res

    def long_conv_pair(toeps, o):
        def tile(h, d):
            return toeps[h][o, (nb - 1 + d) * TOEP:(nb + d) * TOEP, :]
        for h in range(2):
            acc_ref[h] = jnp.dot(lhs_ref[h], tile(h, 0), preferred_element_type=F32)
        for d in range(1, nb):
            keep = (nb - d) * bsz
            for h in range(2):
                acc_ref[h, d * bsz:, :] += jnp.dot(lhs_ref[h, :keep, :], tile(h, d), preferred_element_type=F32)
            for h in range(2):
                acc_ref[h, :keep, :] += jnp.dot(lhs_ref[h, d * bsz:, :], tile(h, -d), preferred_element_type=F32)

    def col(si):
        return slice(si * TOEP, (si + 1) * TOEP)

    def row(si):
        return slice(si * bsz, (si + 1) * bsz)

    def pair_compute(j0, in_slots, out_slots, toeps, ctoeps):
        cs = [c_base + j0 + h for h in range(2)]
        for h in range(2):
            spread(clhs_ref, h, conv_tiles, lambda si, h=h: inbuf[in_slots[h], hy_parts, :, col(si)])
        cvs = [apply(clhs_ref, h, ctoeps[h]) for h in range(2)]
        for h in range(2):
            for si in range(nb):
                outbuf[out_slots[h], 1, :, col(si)] = cvs[h][row(si), :] + dwb_ref[cs[h]]
            out_copy(j0 + h, out_slots[h], 1).start()

        for h in range(2):
            for part in range(hy_parts):
                p = inbuf[in_slots[h], part]
                ch = part * D_HYENA + cs[h]
                before = jnp.where(lane == 0, 0.0, pltpu.roll(p, 1, 1))
                after = jnp.where(lane == seq_len - 1, 0.0, pltpu.roll(p, seq_len - 1, 1))
                conv_ref[h, part] = (sw_ref[ch] * before + sw_ref[hy_parts * D_HYENA + ch] * p
                                     + sw_ref[2 * hy_parts * D_HYENA + ch] * after + sb_ref[ch])
            for si in range(nb):
                lhs_ref[h, row(si), :] = conv_ref[h, 0, :, col(si)].astype(BF16)
        long_conv_pair(toeps, 0)
        for h in range(2):
            skip0 = skip_ref[cs[h]]
            for si in range(nb):
                z = conv_ref[h, 1, :, col(si)] * (acc_ref[h, row(si), :] + conv_ref[h, 0, :, col(si)] * skip0)
                zf_ref[h, row(si), :] = z
                lhs_ref[h, row(si), :] = z.astype(BF16)
        long_conv_pair(toeps, 1)
        for h in range(2):
            skip1 = skip_ref[D_HYENA + cs[h]]
            for si in range(nb):
                outbuf[out_slots[h], 0, :, col(si)] = (conv_ref[h, 2, :, col(si)]
                                                       * (acc_ref[h, row(si), :] + zf_ref[h, row(si), :] * skip1))
            out_copy(j0 + h, out_slots[h], 0).start()

    def in_slots_of(p):
        base = 2 * lax.rem(p, 3)
        return (base, base + 1)

    def pair_stage(p, j0, out_slots, toeps, ctoeps, next_toeps, next_ctoeps, have_prev):
        slots = in_slots_of(p)
        for h in range(2):
            for part in range(parts):
                in_copy(j0 + h, slots[h], part).wait()

        @pl.when(p + 2 < pairs)
        def _():
            ahead = in_slots_of(p + 2)
            for h in range(2):
                for part in range(parts):
                    in_copy(j0 + 4 + h, ahead[h], part).start()

        @pl.when(have_prev)
        def _():
            for h in range(2):
                for kind in range(2):
                    out_copy(j0 - 4 + h, out_slots[h], kind).wait()

        for h in range(2):
            build_strips(next_toeps[h], next_ctoeps[h], jnp.minimum(j0 + 2 + h, chans - 1))
        pair_compute(j0, slots, out_slots, toeps, ctoeps)

    toeps_even, toeps_odd = (toep_a, toep_b), (toep_c, toep_d)
    ctoeps_even, ctoeps_odd = (ctoep_a, ctoep_b), (ctoep_c, ctoep_d)

    def quad(i, carry):
        pair_stage(2 * i, 4 * i, (0, 1), toeps_even, ctoeps_even, toeps_odd, ctoeps_odd, i >= 1)
        pair_stage(2 * i + 1, 4 * i + 2, (2, 3), toeps_odd, ctoeps_odd, toeps_even, ctoeps_even, i >= 1)
        return carry

    for p in range(2):
        for h in range(2):
            for part in range(parts):
                in_copy(2 * p + h, 2 * p + h, part).start()
    for ref in (ctoep_a, ctoep_b, ctoep_c, ctoep_d, clhs_ref):
        ref[...] = jnp.zeros(ref.shape, BF16)
    for h in range(2):
        build_strips(toeps_even[h], ctoeps_even[h], h)
    lax.fori_loop(0, pairs // 2, quad, 0)
    for p in range(pairs - 2, pairs):
        for h in range(2):
            for kind in range(2):
                out_copy(2 * p + h, 2 * (p % 2) + h, kind).wait()


def _stage_b(at, hyt, filt, cfilt, short_w, short_b, skip, dwb):
    bsz, _, seq_len = hyt.shape
    cfg = _tiles(seq_len)
    chans, nb = cfg["ch_b"], cfg["nb"]
    parts = HYENA_ORDER + 2
    long_k = (2 * nb - 1) * TOEP
    conv_k = (2 * (CONV_SPAN // TOEP) - 1) * TOEP
    smem = pl.BlockSpec(memory_space=pltpu.SMEM)
    hbm = pl.BlockSpec(memory_space=pl.ANY)
    chan_major = jax.ShapeDtypeStruct((bsz, D_HYENA, seq_len), F32)
    return pl.pallas_call(
        functools.partial(_stage_b_kernel, bsz=bsz, seq_len=seq_len, cfg=cfg),
        grid=(D_HYENA // chans,),
        in_specs=[smem, smem, smem, smem,
                  pl.BlockSpec((HYENA_ORDER, chans * cfg["long"]["pitch"], LANES), lambda g: (0, g, 0)),
                  pl.BlockSpec((1, chans * cfg["conv"]["pitch"], LANES), lambda g: (0, g, 0)),
                  hbm, hbm],
        out_specs=[hbm, hbm],
        out_shape=[chan_major, chan_major],
        scratch_shapes=[
            pltpu.VMEM((6, parts, bsz, seq_len), F32),
            pltpu.VMEM((4, 2, bsz, seq_len), F32),
            pltpu.VMEM((2, HYENA_ORDER + 1, bsz, seq_len), F32),
            *[pltpu.VMEM((HYENA_ORDER, long_k, TOEP), BF16)] * 4,
            *[pltpu.VMEM((conv_k, TOEP), BF16)] * 4,
            pltpu.VMEM((2, nb * bsz, TOEP), BF16),
            pltpu.VMEM((2, nb * bsz, conv_k), BF16),
            pltpu.VMEM((2, nb * bsz, TOEP), F32),
            pltpu.VMEM((2, nb * bsz, TOEP), F32),
            pltpu.SemaphoreType.DMA((6, parts)),
            pltpu.SemaphoreType.DMA((4, 2)),
        ],
        compiler_params=_cparams(("arbitrary",)),
        name=f"stage_b_{seq_len}",
    )(short_w.reshape(-1), short_b.reshape(-1), skip.reshape(-1), dwb.reshape(-1), filt, cfilt, at, hyt)


def _stage_c_kernel(x_ref, sga_ref, sgb_ref, zbt_ref, cvt_ref, mod_ref, lng_ref, lnb_ref, wpw_ref, whyo_ref,
                    wout_ref, ln1g_ref, ln1b_ref, w1_ref, b1_ref, w2_ref, b2_ref, ln2g_ref, ln2b_ref, o_ref,
                    *, ff_chunk):
    gate1 = mod_ref[0, 2:3, :]
    shift2 = mod_ref[0, 3:4, :]
    scale2 = mod_ref[0, 4:5, :]
    gate2 = mod_ref[0, 5:6, :]
    half = x_ref.shape[1] // 2

    def mixer(rows):
        a2 = _ln_rows(cvt_ref[0, :, rows].T) * lng_ref[...] + lnb_ref[...]
        a2 = (a2 * jax.nn.sigmoid(a2)).astype(BF16)
        ya = jnp.dot(a2, wpw_ref[...], preferred_element_type=F32)
        zb = zbt_ref[0, :, rows].T.astype(BF16)
        yb = jnp.dot(zb, whyo_ref[...], preferred_element_type=F32)
        m = sga_ref[0, rows, :] * ya + sgb_ref[0, rows, :] * yb
        mix = jnp.dot(m.astype(BF16), wout_ref[...], preferred_element_type=F32)
        x1 = _ln_rows(DEEPNORM_ALPHA * x_ref[0, rows, :] + gate1 * mix) * ln1g_ref[...] + ln1b_ref[...]
        return x1, (_ln_rows(x1) * (1.0 + scale2) + shift2).astype(BF16)

    def mlp(x1, h2, rows):
        f = jnp.zeros(x1.shape, F32) + b2_ref[...]
        for j in range(D_FF // ff_chunk):
            cols = slice(j * ff_chunk, (j + 1) * ff_chunk)
            u = jnp.maximum(jnp.dot(h2, w1_ref[:, cols], preferred_element_type=F32) + b1_ref[:, cols], 0.0)
            f = f + jnp.dot((u * u).astype(BF16), w2_ref[cols, :], preferred_element_type=F32)
        o_ref[0, rows, :] = _ln_rows(DEEPNORM_ALPHA * x1 + gate2 * f) * ln2g_ref[...] + ln2b_ref[...]

    rows_a, rows_b = slice(0, half), slice(half, 2 * half)
    x1_a, h2_a = mixer(rows_a)
    x1_b, h2_b = mixer(rows_b)
    mlp(x1_a, h2_a, rows_a)
    mlp(x1_b, h2_b, rows_b)


def _stage_c(x, sga, sgb, zbt, cvt, mod, lng, lnb, wpw, whyo, wout, ln1g, ln1b, w1, b1, w2, b2, ln2g, ln2b):
    bsz, seq_len, _ = x.shape
    tt = _tiles(seq_len)["tt_c"]
    tok = pl.BlockSpec((1, tt, D_MODEL), lambda b, i: (b, i, 0))
    chan = pl.BlockSpec((1, D_HYENA, tt), lambda b, i: (b, 0, i))
    return pl.pallas_call(
        functools.partial(_stage_c_kernel, ff_chunk=D_MODEL),
        grid=(bsz, seq_len // tt),
        in_specs=[tok, tok, tok, chan, chan,
                  pl.BlockSpec((1, N_MOD, D_MODEL), lambda b, i: (b, 0, 0)),
                  _const_spec((1, D_CONV)), _const_spec((1, D_CONV)), _const_spec((D_CONV, D_MODEL)),
                  _const_spec((D_HYENA, D_MODEL)), _const_spec((D_MODEL, D_MODEL)),
                  _const_spec((1, D_MODEL)), _const_spec((1, D_MODEL)),
                  _const_spec((D_MODEL, D_FF)), _const_spec((1, D_FF)),
                  _const_spec((D_FF, D_MODEL)), _const_spec((1, D_MODEL)),
                  _const_spec((1, D_MODEL)), _const_spec((1, D_MODEL))],
        out_specs=tok,
        out_shape=jax.ShapeDtypeStruct((bsz, seq_len, D_MODEL), F32),
        compiler_params=_cparams(("parallel", "parallel")),
        name=f"stage_c_{seq_len}",
    )(x, sga, sgb, zbt, cvt, mod, lng, lnb, wpw, whyo, wout, ln1g, ln1b, w1, b1, w2, b2, ln2g, ln2b)


def _encoder(x, mod, filt, p):
    sga, sgb, at, hyt = _stage_a(x, mod, p["wct"], p["wg"])
    zbt, cvt = _stage_b(at, hyt, filt, p["cfilt"], p["short_w"], p["short_b"], p["skip"], p["dwb"])
    return _stage_c(x, sga, sgb, zbt, cvt, mod, p["lng"], p["lnb"], p["wpw"], p["whyo"], p["wout"],
                    p["ln1g"], p["ln1b"], p["w1"], p["b1"], p["w2"], p["b2"], p["ln2g"], p["ln2b"])


def _prepare(w_in, conv_dw_w, conv_dw_b, conv_ln_g, conv_ln_b, conv_pw_w, hy_short_w, hy_short_b, hy_skip,
             hy_out_w, w_out, ln1_g, ln1_b, mlp_w1, mlp_b1, mlp_w2, mlp_b2, ln2_g, ln2_b, l=0):
    row = lambda v: v.reshape(1, -1)
    s2 = 2 * D_CONV + (HYENA_ORDER + 1) * D_HYENA
    w_in_l = w_in[l]
    return dict(
        wct=w_in_l[:, :s2].T.astype(BF16),
        wg=w_in_l[:, s2:].astype(BF16),
        cfilt=_conv_filter_words(conv_dw_w[l]), dwb=conv_dw_b[l],
        lng=row(conv_ln_g[l]), lnb=row(conv_ln_b[l]), wpw=conv_pw_w[l].astype(BF16),
        short_w=hy_short_w[l], short_b=hy_short_b[l], skip=hy_skip[l],
        whyo=hy_out_w[l].astype(BF16), wout=w_out[l].astype(BF16),
        ln1g=row(ln1_g[l]), ln1b=row(ln1_b[l]),
        w1=mlp_w1[l].astype(BF16), b1=row(mlp_b1[l]), w2=mlp_w2[l].astype(BF16), b2=row(mlp_b2[l]),
        ln2g=row(ln2_g[l]), ln2b=row(ln2_b[l]),
    )


def kernel(x_prompt, x_sample, c_prompt, c_sample, w_ada, b_ada, w_in, conv_dw_w, conv_dw_b, conv_ln_g, conv_ln_b, conv_pw_w, hy_short_w, hy_short_b, hy_ffn_w1, hy_ffn_b1, hy_sin_freq, hy_ffn_w2, hy_ffn_b2, hy_ffn_w3, hy_skip, hy_out_w, w_out, ln1_g, ln1_b, mlp_w1, mlp_b1, mlp_w2, mlp_b2, ln2_g, ln2_b):
    l = 0
    p = _prepare(w_in, conv_dw_w, conv_dw_b, conv_ln_g, conv_ln_b, conv_pw_w, hy_short_w, hy_short_b, hy_skip,
                 hy_out_w, w_out, ln1_g, ln1_b, mlp_w1, mlp_b1, mlp_w2, mlp_b2, ln2_g, ln2_b, l)
    nb_p = x_prompt.shape[0]
    mod = _mod(jnp.concatenate([c_prompt, c_sample], axis=0), w_ada[l], b_ada[l])
    mod = mod.reshape(-1, N_MOD, D_MODEL)
    outs = []
    for x, m in ((x_prompt, mod[:nb_p]), (x_sample, mod[nb_p:])):
        filt = _filters(x.shape[1], hy_ffn_w1[l], hy_ffn_b1[l], hy_sin_freq[l],
                        hy_ffn_w2[l], hy_ffn_b2[l], hy_ffn_w3[l])
        outs.append(_encoder(x, m, filt, p))
    return tuple(outs)
```

```python
import functools
import math

import jax
import jax.numpy as jnp
from jax import lax
from jax.experimental import pallas as pl
from jax.experimental.pallas import tpu as pltpu

F32 = jnp.float32
BF16 = jnp.bfloat16
HIGHEST = lax.Precision.HIGHEST

LANES = 128
SUBLANES = 8
MXU_DIM = 256
VMEM_LIMIT_BYTES = 56 * 1024 * 1024

D_MODEL = 1024
D_CONV = D_MODEL // 2
D_HYENA = D_MODEL // 2
CONV_KERNEL = 31
SHORT_KERNEL = 3
HYENA_ORDER = 2
N_BANDS = 16
FILTER_EMB = 1 + 2 * N_BANDS
FILTER_HIDDEN = 64
N_DIRS = 2
D_FF = 4 * D_MODEL
N_MOD = 6
LN_EPS = 1e-5
DEPTH = 1
MAX_DECAY = math.log(1e-2) / 0.3
MIN_DECAY = math.log(1e-2) / 1.5
DEEPNORM_ALPHA = (2.0 * DEPTH) ** 0.25

TOEP = MXU_DIM
CONV_SPAN = 2 * TOEP
NT_DIMS = (((1,), (1,)), ((), ()))


def _strip_geometry(span):
    width = 2 * span + TOEP
    slabs = width // LANES
    return dict(width=width, slabs=slabs, pitch=-(-slabs // SUBLANES) * SUBLANES, centre=span + TOEP - 1)


def _tiles(seq_len):
    return dict(
        tt_a=512,
        tt_c=512,
        ch_b=32,
        filt_rows=128,
        nb=seq_len // TOEP,
        long=_strip_geometry(seq_len),
        conv=_strip_geometry(CONV_SPAN),
    )


def _cparams(sem):
    return pltpu.CompilerParams(dimension_semantics=sem, vmem_limit_bytes=VMEM_LIMIT_BYTES)


def _ln_rows(x):
    mu = jnp.mean(x, axis=-1, keepdims=True)
    xc = x - mu
    var = jnp.mean(xc * xc, axis=-1, keepdims=True)
    return xc * lax.rsqrt(var + LN_EPS)


def _mod_kernel(c_ref, w_ref, b_ref, o_ref):
    c = c_ref[...]
    s = c * jax.nn.sigmoid(c)
    o_ref[...] = jnp.dot(s, w_ref[...], precision=HIGHEST, preferred_element_type=F32) + b_ref[...]


def _mod(c_all, w_ada, b_ada):
    rows = c_all.shape[0]
    n = w_ada.shape[1]
    bn = D_MODEL
    return pl.pallas_call(
        _mod_kernel,
        grid=(n // bn,),
        in_specs=[pl.BlockSpec((rows, D_MODEL), lambda j: (0, 0)),
                  pl.BlockSpec((D_MODEL, bn), lambda j: (0, j)),
                  pl.BlockSpec((1, bn), lambda j: (0, j))],
        out_specs=pl.BlockSpec((rows, bn), lambda j: (0, j)),
        out_shape=jax.ShapeDtypeStruct((rows, n), F32),
        compiler_params=_cparams(("parallel",)),
        name="adaln_mod",
    )(c_all, w_ada, b_ada.reshape(1, n))


def _bf16_bits(x):
    b = lax.bitcast_convert_type(x, jnp.uint32)
    return (b + jnp.uint32(0x7FFF) + ((b >> 16) & jnp.uint32(1))) >> 16


def _filter_kernel(w1t_ref, b1_ref, fr_ref, w2t_ref, b2_ref, w3t_ref, o_ref, hdn_ref, *, seq_len, cfg):
    geo, rows = cfg["long"], cfg["filt_rows"]
    width, slabs, pitch, centre = geo["width"], geo["slabs"], geo["pitch"], geo["centre"]
    m = lax.broadcasted_iota(jnp.int32, (1, width), 1)
    tau = jnp.abs(m - centre).astype(F32)
    t = tau * (1.0 / (seq_len - 1))

    @pl.when(pl.program_id(0) == 0)
    def _():
        w = tau * (2.0 * math.pi / seq_len)
        bidx = lax.broadcasted_iota(jnp.int32, (N_BANDS, 1), 0).astype(F32)
        bands = 1e-4 + bidx * ((N_BANDS - 1 - 1e-4) / (N_BANDS - 1))
        ang = bands * w
        z = jnp.concatenate([t, jnp.cos(ang), -jnp.sin(ang)], axis=0)
        fr = fr_ref[...]
        h1 = jnp.sin(fr * (jnp.dot(w1t_ref[...], z, precision=HIGHEST, preferred_element_type=F32) + b1_ref[...]))
        hdn_ref[...] = jnp.sin(fr * (jnp.dot(w2t_ref[...], h1, precision=HIGHEST, preferred_element_type=F32)
                                     + b2_ref[...]))

    c0 = pl.program_id(0) * rows
    cidx = (lax.broadcasted_iota(jnp.int32, (rows, 1), 0) + c0).astype(F32)
    step = (MAX_DECAY - MIN_DECAY) / (D_HYENA - 1)
    delta_f = jnp.abs(MIN_DECAY + cidx * step)
    delta_b = jnp.abs(MIN_DECAY + (D_HYENA - 1 - cidx) * step)
    fwd = m >= centre
    decay = jnp.exp(-t * jnp.where(fwd, delta_f, delta_b))
    valid = tau <= float(seq_len - 1)
    hdn = hdn_ref[...]
    for o in range(HYENA_ORDER):
        hf = jnp.dot(w3t_ref[o, 0], hdn, precision=HIGHEST, preferred_element_type=F32)
        hb = jnp.dot(w3t_ref[o, 1], hdn, precision=HIGHEST, preferred_element_type=F32)
        k = jnp.where(valid, jnp.where(fwd, hf, hb) * decay, 0.0)
        word = _bf16_bits(k) | (_bf16_bits(pltpu.roll(k, 1, 1)) << 16)
        for s in range(slabs):
            o_ref[o, pl.ds(s, rows, stride=pitch), :] = word[:, s * LANES:(s + 1) * LANES]
        for s in range(slabs, pitch):
            o_ref[o, pl.ds(s, rows, stride=pitch), :] = jnp.zeros((rows, LANES), jnp.uint32)


def _filters(seq_len, w1, b1, freq, w2, b2, w3):
    cfg = _tiles(seq_len)
    rows, pitch, width = cfg["filt_rows"], cfg["long"]["pitch"], cfg["long"]["width"]
    w3t = w3.T.reshape(HYENA_ORDER, N_DIRS, D_HYENA, FILTER_HIDDEN)
    col = lambda v: v.reshape(FILTER_HIDDEN, 1)
    full = lambda shape: pl.BlockSpec(shape, lambda i: (0,) * len(shape))
    return pl.pallas_call(
        functools.partial(_filter_kernel, seq_len=seq_len, cfg=cfg),
        grid=(D_HYENA // rows,),
        in_specs=[full((FILTER_HIDDEN, FILTER_EMB)), full((FILTER_HIDDEN, 1)), full((FILTER_HIDDEN, 1)),
                  full((FILTER_HIDDEN, FILTER_HIDDEN)), full((FILTER_HIDDEN, 1)),
                  pl.BlockSpec((HYENA_ORDER, N_DIRS, rows, FILTER_HIDDEN), lambda i: (0, 0, i, 0))],
        out_specs=pl.BlockSpec((HYENA_ORDER, rows * pitch, LANES), lambda i: (0, i, 0)),
        out_shape=jax.ShapeDtypeStruct((HYENA_ORDER, D_HYENA * pitch, LANES), jnp.uint32),
        scratch_shapes=[pltpu.VMEM((FILTER_HIDDEN, width), F32)],
        compiler_params=_cparams(("arbitrary",)),
        name=f"hyena_filters_{seq_len}",
    )(w1.T, col(b1), col(freq), w2.T, col(b2), w3t)


def _conv_filter_words(dww):
    geo = _tiles(CONV_SPAN)["conv"]
    pad = CONV_KERNEL // 2
    tap = pad - (jnp.arange(geo["width"]) - geo["centre"])
    k = jnp.where((tap >= 0) & (tap < CONV_KERNEL), dww.T[:, jnp.clip(tap, 0, CONV_KERNEL - 1)], 0.0)
    bits = lambda v: lax.bitcast_convert_type(v.astype(BF16), jnp.uint16).astype(jnp.uint32)
    word = bits(k) | (bits(jnp.roll(k, 1, axis=1)) << 16)
    word = word.reshape(D_CONV, geo["slabs"], LANES)
    word = jnp.pad(word, ((0, 0), (0, geo["pitch"] - geo["slabs"]), (0, 0)))
    return word.reshape(1, D_CONV * geo["pitch"], LANES)


def _stage_a_kernel(x_ref, mod_ref, wct_ref, wg_ref, sga_ref, sgb_ref, at_ref, hyt_ref):
    shift1 = mod_ref[0, 0:1, :]
    scale1 = mod_ref[0, 1:2, :]
    half = x_ref.shape[1] // 2
    hs = [(_ln_rows(x_ref[0, r:r + half, :]) * (1.0 + scale1) + shift1).astype(BF16) for r in (0, half)]
    for i, h in enumerate(hs):
        rows = slice(i * half, (i + 1) * half)
        ct = lax.dot_general(wct_ref[...], h, NT_DIMS, preferred_element_type=F32)
        at_ref[0, :, rows] = ct[:D_CONV] * jax.nn.sigmoid(ct[D_CONV:2 * D_CONV])
        hyt_ref[0, :, rows] = ct[2 * D_CONV:]
        for c in range(2):
            cols = slice(c * D_CONV, (c + 1) * D_CONV)
            sga_ref[0, rows, cols] = jax.nn.sigmoid(jnp.dot(h, wg_ref[:, cols], preferred_element_type=F32))
            colsb = slice(D_MODEL + c * D_CONV, D_MODEL + (c + 1) * D_CONV)
            sgb_ref[0, rows, cols] = jax.nn.sigmoid(jnp.dot(h, wg_ref[:, colsb], preferred_element_type=F32))


def _const_spec(shape):
    return pl.BlockSpec(shape, lambda b, i: (0,) * len(shape), pipeline_mode=pl.Buffered(1))


def _stage_a(x, mod, wct, wg):
    bsz, seq_len, _ = x.shape
    tt = _tiles(seq_len)["tt_a"]
    hy_rows = (HYENA_ORDER + 1) * D_HYENA
    tok = pl.BlockSpec((1, tt, D_MODEL), lambda b, i: (b, i, 0))
    return pl.pallas_call(
        _stage_a_kernel,
        grid=(bsz, seq_len // tt),
        in_specs=[tok,
                  pl.BlockSpec((1, N_MOD, D_MODEL), lambda b, i: (b, 0, 0)),
                  _const_spec((2 * D_CONV + hy_rows, D_MODEL)),
                  _const_spec((D_MODEL, 2 * D_MODEL))],
        out_specs=[tok, tok,
                   pl.BlockSpec((1, D_CONV, tt), lambda b, i: (b, 0, i)),
                   pl.BlockSpec((1, hy_rows, tt), lambda b, i: (b, 0, i))],
        out_shape=[jax.ShapeDtypeStruct((bsz, seq_len, D_MODEL), F32),
                   jax.ShapeDtypeStruct((bsz, seq_len, D_MODEL), F32),
                   jax.ShapeDtypeStruct((bsz, D_CONV, seq_len), F32),
                   jax.ShapeDtypeStruct((bsz, hy_rows, seq_len), F32)],
        compiler_params=_cparams(("parallel", "parallel")),
        name=f"stage_a_{seq_len}",
    )(x, mod, wct, wg)


def _stage_b_kernel(sw_ref, sb_ref, skip_ref, dwb_ref, filt_ref, cfilt_ref, at_hbm, hyt_hbm, zb_hbm, cv_hbm,
                    inbuf, outbuf, conv_ref, toep_a, toep_b, toep_c, toep_d, ctoep_a, ctoep_b, ctoep_c, ctoep_d,
                    lhs_ref, clhs_ref, zf_ref, acc_ref, in_sem, out_sem, *, bsz, seq_len, cfg):
    nb, chans = cfg["nb"], cfg["ch_b"]
    geo, cgeo = cfg["long"], cfg["conv"]
    hy_parts = HYENA_ORDER + 1
    parts = hy_parts + 1
    pairs = chans // 2
    c_base = pl.program_id(0) * chans

    def in_copy(j, slot, part):
        c = c_base + j
        src = at_hbm.at[:, c, :] if part == hy_parts else hyt_hbm.at[:, part * D_HYENA + c, :]
        return pltpu.make_async_copy(src, inbuf.at[slot, part], in_sem.at[slot, part])

    def out_copy(j, slot, kind):
        dst = (zb_hbm, cv_hbm)[kind].at[:, c_base + j, :]
        return pltpu.make_async_copy(outbuf.at[slot, kind], dst, out_sem.at[slot, kind])

    lane = lax.broadcasted_iota(jnp.int32, (bsz, seq_len), 1)
    half = LANES // 2
    word_row = lax.broadcasted_iota(jnp.int32, (half, LANES), 0)
    col = lax.broadcasted_iota(jnp.int32, (half, LANES), 1)
    low = col <= 2 * word_row

    def build_strip(dst_ref, src_ref, row0, k_lo, k_hi, n_tiles):
        def put(offset, row_half, big):
            q, within = divmod(offset, TOEP)
            if 1 <= q <= n_tiles:
                r0 = (q - 1) * TOEP + row_half * LANES
                dst_ref[r0:r0 + LANES, within:within + LANES] = big

        prev = None
        for k in range(k_lo, k_hi + 1):
            fb = jnp.broadcast_to(src_ref[pl.ds(row0 + k, 1), :], (half, LANES))
            cur = pltpu.roll(fb, 1, 1, stride=2, stride_axis=0)
            if prev is not None:
                big = pltpu.bitcast(jnp.where(low, prev, cur), BF16)
                put((k - 1) * LANES, 1, big)
                if k >= 2:
                    put((k - 2) * LANES, 0, big)
            prev = cur

    ck_lo = (cgeo["centre"] - CONV_KERNEL // 2) // LANES - 1
    ck_hi = (cgeo["centre"] + CONV_KERNEL // 2) // LANES + 1
    long_tiles = 2 * nb - 1
    conv_tiles = 2 * (CONV_SPAN // TOEP) - 1

    def build_strips(toep_ref, ctoep_ref, j):
        for o in range(HYENA_ORDER):
            build_strip(toep_ref.at[o], filt_ref.at[o], j * geo["pitch"], 0, geo["slabs"] - 1, long_tiles)
        build_strip(ctoep_ref, cfilt_ref.at[0], j * cgeo["pitch"], ck_lo, ck_hi, conv_tiles)

    def spread(dst_ref, h, n_tiles, block):
        reach = n_tiles // 2
        for si in range(nb):
            blk = block(si).astype(BF16)
            for q in range(n_tiles):
                tj = si + q - reach
                if 0 <= tj < nb:
                    dst_ref[h, tj * bsz:(tj + 1) * bsz, q * TOEP:(q + 1) * TOEP] = blk

    def apply(src_ref, h, stack):
        k = src_ref.shape[-1]
        ks = -(-(k // TOEP) // 2) * TOEP
        res = jnp.dot(src_ref[h, :, :ks], stack[:ks, :], preferred_element_type=F32)
        if ks < k:
            res = res + jnp.dot(src_ref[h, :, ks:], stack[ks:, :], preferred_element_type=F32)
        return res

    def long_conv_pair(toeps, o):
        def tile(h, d):
            return toeps[h][o, (nb - 1 + d) * TOEP:(nb + d) * TOEP, :]
        for h in range(2):
            acc_ref[h] = jnp.dot(lhs_ref[h], tile(h, 0), preferred_element_type=F32)
        for d in range(1, nb):
            keep = (nb - d) * bsz
            for h in range(2):
                acc_ref[h, d * bsz:, :] += jnp.dot(lhs_ref[h, :keep, :], tile(h, d), preferred_element_type=F32)
            for h in range(2):
                acc_ref[h, :keep, :] += jnp.dot(lhs_ref[h, d * bsz:, :], tile(h, -d), preferred_element_type=F32)

    def col(si):
        return slice(si * TOEP, (si + 1) * TOEP)

    def row(si):
        return slice(si * bsz, (si + 1) * bsz)

    def pair_compute(j0, in_slots, out_slots, toeps, ctoeps):
        cs = [c_base + j0 + h for h in range(2)]
        for h in range(2):
            spread(clhs_ref, h, conv_tiles, lambda si, h=h: inbuf[in_slots[h], hy_parts, :, col(si)])
        cvs = [apply(clhs_ref, h, ctoeps[h]) for h in range(2)]
        for h in range(2):
            for si in range(nb):
                outbuf[out_slots[h], 1, :, col(si)] = cvs[h][row(si), :] + dwb_ref[cs[h]]
            out_copy(j0 + h, out_slots[h], 1).start()

        for h in range(2):
            for part in range(hy_parts):
                p = inbuf[in_slots[h], part]
                ch = part * D_HYENA + cs[h]
                before = jnp.where(lane == 0, 0.0, pltpu.roll(p, 1, 1))
                after = jnp.where(lane == seq_len - 1, 0.0, pltpu.roll(p, seq_len - 1, 1))
                conv_ref[h, part] = (sw_ref[ch] * before + sw_ref[hy_parts * D_HYENA + ch] * p
                                     + sw_ref[2 * hy_parts * D_HYENA + ch] * after + sb_ref[ch])
            for si in range(nb):
                lhs_ref[h, row(si), :] = conv_ref[h, 0, :, col(si)].astype(BF16)
        long_conv_pair(toeps, 0)
        for h in range(2):
            skip0 = skip_ref[cs[h]]
            for si in range(nb):
                z = conv_ref[h, 1, :, col(si)] * (acc_ref[h, row(si), :] + conv_ref[h, 0, :, col(si)] * skip0)
                zf_ref[h, row(si), :] = z
                lhs_ref[h, row(si), :] = z.astype(BF16)
        long_conv_pair(toeps, 1)
        for h in range(2):
            skip1 = skip_ref[D_HYENA + cs[h]]
            for si in range(nb):
                outbuf[out_slots[h], 0, :, col(si)] = (conv_ref[h, 2, :, col(si)]
                                                       * (acc_ref[h, row(si), :] + zf_ref[h, row(si), :] * skip1))
            out_copy(j0 + h, out_slots[h], 0).start()

    def in_slots_of(p):
        base = 2 * lax.rem(p, 3)
        return (base, base + 1)

    def pair_stage(p, j0, out_slots, toeps, ctoeps, next_toeps, next_ctoeps, have_prev):
        slots = in_slots_of(p)
        for h in range(2):
            for part in range(parts):
                in_copy(j0 + h, slots[h], part).wait()

        @pl.when(p + 2 < pairs)
        def _():
            ahead = in_slots_of(p + 2)
            for h in range(2):
                for part in range(parts):
                    in_copy(j0 + 4 + h, ahead[h], part).start()

        @pl.when(have_prev)
        def _():
            for h in range(2):
                for kind in range(2):
                    out_copy(j0 - 4 + h, out_slots[h], kind).wait()

        for h in range(2):
            build_strips(next_toeps[h], next_ctoeps[h], jnp.minimum(j0 + 2 + h, chans - 1))
        pair_compute(j0, slots, out_slots, toeps, ctoeps)

    toeps_even, toeps_odd = (toep_a, toep_b), (toep_c, toep_d)
    ctoeps_even, ctoeps_odd = (ctoep_a, ctoep_b), (ctoep_c, ctoep_d)

    def quad(i, carry):
        pair_stage(2 * i, 4 * i, (0, 1), toeps_even, ctoeps_even, toeps_odd, ctoeps_odd, i >= 1)
        pair_stage(2 * i + 1, 4 * i + 2, (2, 3), toeps_odd, ctoeps_odd, toeps_even, ctoeps_even, i >= 1)
        return carry

    for p in range(2):
        for h in range(2):
            for part in range(parts):
                in_copy(2 * p + h, 2 * p + h, part).start()
    for ref in (ctoep_a, ctoep_b, ctoep_c, ctoep_d, clhs_ref):
        ref[...] = jnp.zeros(ref.shape, BF16)
    for h in range(2):
        build_strips(toeps_even[h], ctoeps_even[h], h)
    lax.fori_loop(0, pairs // 2, quad, 0)
    for p in range(pairs - 2, pairs):
        for h in range(2):
            for kind in range(2):
                out_copy(2 * p + h, 2 * (p % 2) + h, kind).wait()


def _stage_b(at, hyt, filt, cfilt, short_w, short_b, skip, dwb):
    bsz, _, seq_len = hyt.shape
    cfg = _tiles(seq_len)
    chans, nb = cfg["ch_b"], cfg["nb"]
    parts = HYENA_ORDER + 2
    long_k = (2 * nb - 1) * TOEP
    conv_k = (2 * (CONV_SPAN // TOEP) - 1) * TOEP
    smem = pl.BlockSpec(memory_space=pltpu.SMEM)
    hbm = pl.BlockSpec(memory_space=pl.ANY)
    chan_major = jax.ShapeDtypeStruct((bsz, D_HYENA, seq_len), F32)
    return pl.pallas_call(
        functools.partial(_stage_b_kernel, bsz=bsz, seq_len=seq_len, cfg=cfg),
        grid=(D_HYENA // chans,),
        in_specs=[smem, smem, smem, smem,
                  pl.BlockSpec((HYENA_ORDER, chans * cfg["long"]["pitch"], LANES), lambda g: (0, g, 0)),
                  pl.BlockSpec((1, chans * cfg["conv"]["pitch"], LANES), lambda g: (0, g, 0)),
                  hbm, hbm],
        out_specs=[hbm, hbm],
        out_shape=[chan_major, chan_major],
        scratch_shapes=[
            pltpu.VMEM((6, parts, bsz, seq_len), F32),
            pltpu.VMEM((4, 2, bsz, seq_len), F32),
            pltpu.VMEM((2, HYENA_ORDER + 1, bsz, seq_len), F32),
            *[pltpu.VMEM((HYENA_ORDER, long_k, TOEP), BF16)] * 4,
            *[pltpu.VMEM((conv_k, TOEP), BF16)] * 4,
            pltpu.VMEM((2, nb * bsz, TOEP), BF16),
            pltpu.VMEM((2, nb * bsz, conv_k), BF16),
            pltpu.VMEM((2, nb * bsz, TOEP), F32),
            pltpu.VMEM((2, nb * bsz, TOEP), F32),
            pltpu.SemaphoreType.DMA((6, parts)),
            pltpu.SemaphoreType.DMA((4, 2)),
        ],
        compiler_params=_cparams(("arbitrary",)),
        name=f"stage_b_{seq_len}",
    )(short_w.reshape(-1), short_b.reshape(-1), skip.reshape(-1), dwb.reshape(-1), filt, cfilt, at, hyt)


def _stage_c_kernel(x_ref, sga_ref, sgb_ref, zbt_ref, cvt_ref, mod_ref, lng_ref, lnb_ref, wpw_ref, whyo_ref,
                    wout_ref, ln1g_ref, ln1b_ref, w1_ref, b1_ref, w2_ref, b2_ref, ln2g_ref, ln2b_ref, o_ref,
                    *, ff_chunk):
    gate1 = mod_ref[0, 2:3, :]
    shift2 = mod_ref[0, 3:4, :]
    scale2 = mod_ref[0, 4:5, :]
    gate2 = mod_ref[0, 5:6, :]
    half = x_ref.shape[1] // 2

    def mixer(rows):
        a2 = _ln_rows(cvt_ref[0, :, rows].T) * lng_ref[...] + lnb_ref[...]
        a2 = (a2 * jax.nn.sigmoid(a2)).astype(BF16)
        ya = jnp.dot(a2, wpw_ref[...], preferred_element_type=F32)
        zb = zbt_ref[0, :, rows].T.astype(BF16)
        yb = jnp.dot(zb, whyo_ref[...], preferred_element_type=F32)
        m = sga_ref[0, rows, :] * ya + sgb_ref[0, rows, :] * yb
        mix = jnp.dot(m.astype(BF16), wout_ref[...], preferred_element_type=F32)
        x1 = _ln_rows(DEEPNORM_ALPHA * x_ref[0, rows, :] + gate1 * mix) * ln1g_ref[...] + ln1b_ref[...]
        return x1, (_ln_rows(x1) * (1.0 + scale2) + shift2).astype(BF16)

    def mlp(x1, h2, rows):
        f = jnp.zeros(x1.shape, F32) + b2_ref[...]
        for j in range(D_FF // ff_chunk):
            cols = slice(j * ff_chunk, (j + 1) * ff_chunk)
            u = jnp.maximum(jnp.dot(h2, w1_ref[:, cols], preferred_element_type=F32) + b1_ref[:, cols], 0.0)
            f = f + jnp.dot((u * u).astype(BF16), w2_ref[cols, :], preferred_element_type=F32)
        o_ref[0, rows, :] = _ln_rows(DEEPNORM_ALPHA * x1 + gate2 * f) * ln2g_ref[...] + ln2b_ref[...]

    rows_a, rows_b = slice(0, half), slice(half, 2 * half)
    x1_a, h2_a = mixer(rows_a)
    x1_b, h2_b = mixer(rows_b)
    mlp(x1_a, h2_a, rows_a)
    mlp(x1_b, h2_b, rows_b)


def _stage_c(x, sga, sgb, zbt, cvt, mod, lng, lnb, wpw, whyo, wout, ln1g, ln1b, w1, b1, w2, b2, ln2g, ln2b):
    bsz, seq_len, _ = x.shape
    tt = _tiles(seq_len)["tt_c"]
    tok = pl.BlockSpec((1, tt, D_MODEL), lambda b, i: (b, i, 0))
    chan = pl.BlockSpec((1, D_HYENA, tt), lambda b, i: (b, 0, i))
    return pl.pallas_call(
        functools.partial(_stage_c_kernel, ff_chunk=D_MODEL),
        grid=(bsz, seq_len // tt),
        in_specs=[tok, tok, tok, chan, chan,
                  pl.BlockSpec((1, N_MOD, D_MODEL), lambda b, i: (b, 0, 0)),
                  _const_spec((1, D_CONV)), _const_spec((1, D_CONV)), _const_spec((D_CONV, D_MODEL)),
                  _const_spec((D_HYENA, D_MODEL)), _const_spec((D_MODEL, D_MODEL)),
                  _const_spec((1, D_MODEL)), _const_spec((1, D_MODEL)),
                  _const_spec((D_MODEL, D_FF)), _const_spec((1, D_FF)),
                  _const_spec((D_FF, D_MODEL)), _const_spec((1, D_MODEL)),
                  _const_spec((1, D_MODEL)), _const_spec((1, D_MODEL))],
        out_specs=tok,
        out_shape=jax.ShapeDtypeStruct((bsz, seq_len, D_MODEL), F32),
        compiler_params=_cparams(("parallel", "parallel")),
        name=f"stage_c_{seq_len}",
    )(x, sga, sgb, zbt, cvt, mod, lng, lnb, wpw, whyo, wout, ln1g, ln1b, w1, b1, w2, b2, ln2g, ln2b)


def _encoder(x, mod, filt, p):
    sga, sgb, at, hyt = _stage_a(x, mod, p["wct"], p["wg"])
    zbt, cvt = _stage_b(at, hyt, filt, p["cfilt"], p["short_w"], p["short_b"], p["skip"], p["dwb"])
    return _stage_c(x, sga, sgb, zbt, cvt, mod, p["lng"], p["lnb"], p["wpw"], p["whyo"], p["wout"],
                    p["ln1g"], p["ln1b"], p["w1"], p["b1"], p["w2"], p["b2"], p["ln2g"], p["ln2b"])


def _prepare(w_in, conv_dw_w, conv_dw_b, conv_ln_g, conv_ln_b, conv_pw_w, hy_short_w, hy_short_b, hy_skip,
             hy_out_w, w_out, ln1_g, ln1_b, mlp_w1, mlp_b1, mlp_w2, mlp_b2, ln2_g, ln2_b, l=0):
    row = lambda v: v.reshape(1, -1)
    s2 = 2 * D_CONV + (HYENA_ORDER + 1) * D_HYENA
    w_in_l = w_in[l]
    return dict(
        wct=w_in_l[:, :s2].T.astype(BF16),
        wg=w_in_l[:, s2:].astype(BF16),
        cfilt=_conv_filter_words(conv_dw_w[l]), dwb=conv_dw_b[l],
        lng=row(conv_ln_g[l]), lnb=row(conv_ln_b[l]), wpw=conv_pw_w[l].astype(BF16),
        short_w=hy_short_w[l], short_b=hy_short_b[l], skip=hy_skip[l],
        whyo=hy_out_w[l].astype(BF16), wout=w_out[l].astype(BF16),
        ln1g=row(ln1_g[l]), ln1b=row(ln1_b[l]),
        w1=mlp_w1[l].astype(BF16), b1=row(mlp_b1[l]), w2=mlp_w2[l].astype(BF16), b2=row(mlp_b2[l]),
        ln2g=row(ln2_g[l]), ln2b=row(ln2_b[l]),
    )


def kernel(x_prompt, x_sample, c_prompt, c_sample, w_ada, b_ada, w_in, conv_dw_w, conv_dw_b, conv_ln_g, conv_ln_b, conv_pw_w, hy_short_w, hy_short_b, hy_ffn_w1, hy_ffn_b1, hy_sin_freq, hy_ffn_w2, hy_ffn_b2, hy_ffn_w3, hy_skip, hy_out_w, w_out, ln1_g, ln1_b, mlp_w1, mlp_b1, mlp_w2, mlp_b2, ln2_g, ln2_b):
    l = 0
    p = _prepare(w_in, conv_dw_w, conv_dw_b, conv_ln_g, conv_ln_b, conv_pw_w, hy_short_w, hy_short_b, hy_skip,
                 hy_out_w, w_out, ln1_g, ln1_b, mlp_w1, mlp_b1, mlp_w2, mlp_b2, ln2_g, ln2_b, l)
    nb_p = x_prompt.shape[0]
    mod = _mod(jnp.concatenate([c_prompt, c_sample], axis=0), w_ada[l], b_ada[l])
    mod = mod.reshape(-1, N_MOD, D_MODEL)
    outs = []
    for x, m in ((x_prompt, mod[:nb_p]), (x_sample, mod[nb_p:])):
        filt = _filters(x.shape[1], hy_ffn_w1[l], hy_ffn_b1[l], hy_sin_freq[l],
                        hy_ffn_w2[l], hy_ffn_b2[l], hy_ffn_w3[l])
        outs.append(_encoder(x, m, filt, p))
    return tuple(outs)
```

```python
import functools
import math

import jax
import jax.numpy as jnp
from jax import lax
from jax.experimental import pallas as pl
from jax.experimental.pallas import tpu as pltpu

F32 = jnp.float32
BF16 = jnp.bfloat16
HIGHEST = lax.Precision.HIGHEST

LANES = 128
SUBLANES = 8
MXU_DIM = 256
VMEM_LIMIT_BYTES = 56 * 1024 * 1024

D_MODEL = 1024
D_CONV = D_MODEL // 2
D_HYENA = D_MODEL // 2
CONV_KERNEL = 31
SHORT_KERNEL = 3
HYENA_ORDER = 2
N_BANDS = 16
FILTER_EMB = 1 + 2 * N_BANDS
FILTER_HIDDEN = 64
N_DIRS = 2
D_FF = 4 * D_MODEL
N_MOD = 6
LN_EPS = 1e-5
DEPTH = 1
MAX_DECAY = math.log(1e-2) / 0.3
MIN_DECAY = math.log(1e-2) / 1.5
DEEPNORM_ALPHA = (2.0 * DEPTH) ** 0.25

TOEP = MXU_DIM
CONV_SPAN = 2 * TOEP
NT_DIMS = (((1,), (1,)), ((), ()))


def _strip_geometry(span):
    width = 2 * span + TOEP
    slabs = width // LANES
    return dict(width=width, slabs=slabs, pitch=-(-slabs // SUBLANES) * SUBLANES, centre=span + TOEP - 1)


def _tiles(seq_len):
    return dict(
        tt_a=512,
        tt_c=512,
        ch_b=32,
        filt_rows=128,
        nb=seq_len // TOEP,
        long=_strip_geometry(seq_len),
        conv=_strip_geometry(CONV_SPAN),
    )


def _cparams(sem):
    return pltpu.CompilerParams(dimension_semantics=sem, vmem_limit_bytes=VMEM_LIMIT_BYTES)


def _ln_rows(x):
    mu = jnp.mean(x, axis=-1, keepdims=True)
    xc = x - mu
    var = jnp.mean(xc * xc, axis=-1, keepdims=True)
    return xc * lax.rsqrt(var + LN_EPS)


def _mod_kernel(c_ref, w_ref, b_ref, o_ref):
    c = c_ref[...]
    s = c * jax.nn.sigmoid(c)
    o_ref[...] = jnp.dot(s, w_ref[...], precision=HIGHEST, preferred_element_type=F32) + b_ref[...]


def _mod(c_all, w_ada, b_ada):
    rows = c_all.shape[0]
    n = w_ada.shape[1]
    bn = D_MODEL
    return pl.pallas_call(
        _mod_kernel,
        grid=(n // bn,),
        in_specs=[pl.BlockSpec((rows, D_MODEL), lambda j: (0, 0)),
                  pl.BlockSpec((D_MODEL, bn), lambda j: (0, j)),
                  pl.BlockSpec((1, bn), lambda j: (0, j))],
        out_specs=pl.BlockSpec((rows, bn), lambda j: (0, j)),
        out_shape=jax.ShapeDtypeStruct((rows, n), F32),
        compiler_params=_cparams(("parallel",)),
        name="adaln_mod",
    )(c_all, w_ada, b_ada.reshape(1, n))


def _bf16_bits(x):
    b = lax.bitcast_convert_type(x, jnp.uint32)
    return (b + jnp.uint32(0x7FFF) + ((b >> 16) & jnp.uint32(1))) >> 16


def _split_bf16(x):
    hi = x.astype(BF16)
    return hi, (x - hi.astype(F32)).astype(BF16)


def _filter_kernel(w1t_ref, b1_ref, fr_ref, w2t_ref, b2_ref, w3t_ref, o_ref, hdn_hi_ref, hdn_lo_ref,
                   *, seq_len, cfg):
    geo, rows = cfg["long"], cfg["filt_rows"]
    width, slabs, pitch, centre = geo["width"], geo["slabs"], geo["pitch"], geo["centre"]
    m = lax.broadcasted_iota(jnp.int32, (1, width), 1)
    tau = jnp.abs(m - centre).astype(F32)
    t = tau * (1.0 / (seq_len - 1))

    @pl.when(pl.program_id(0) == 0)
    def _():
        w = tau * (2.0 * math.pi / seq_len)
        bidx = lax.broadcasted_iota(jnp.int32, (N_BANDS, 1), 0).astype(F32)
        bands = 1e-4 + bidx * ((N_BANDS - 1 - 1e-4) / (N_BANDS - 1))
        ang = bands * w
        z = jnp.concatenate([t, jnp.cos(ang), -jnp.sin(ang)], axis=0)
        fr = fr_ref[...]
        h1 = jnp.sin(fr * (jnp.dot(w1t_ref[...], z, precision=HIGHEST, preferred_element_type=F32) + b1_ref[...]))
        hdn = jnp.sin(fr * (jnp.dot(w2t_ref[...], h1, precision=HIGHEST, preferred_element_type=F32) + b2_ref[...]))
        hdn_hi_ref[...], hdn_lo_ref[...] = _split_bf16(hdn)

    c0 = pl.program_id(0) * rows
    cidx = (lax.broadcasted_iota(jnp.int32, (rows, 1), 0) + c0).astype(F32)
    step = (MAX_DECAY - MIN_DECAY) / (D_HYENA - 1)
    delta_f = jnp.abs(MIN_DECAY + cidx * step)
    delta_b = jnp.abs(MIN_DECAY + (D_HYENA - 1 - cidx) * step)
    fwd = m >= centre
    decay = jnp.exp(-t * jnp.where(fwd, delta_f, delta_b))
    valid = tau <= float(seq_len - 1)
    both = centre // LANES * LANES

    def project(wt, lo, hi):
        w_hi, w_lo = _split_bf16(wt)
        h_hi, h_lo = hdn_hi_ref[:, lo:hi], hdn_lo_ref[:, lo:hi]
        dot = functools.partial(jnp.dot, preferred_element_type=F32)
        return dot(w_hi, h_hi) + dot(w_hi, h_lo) + dot(w_lo, h_hi)

    for o in range(HYENA_ORDER):
        hb = project(w3t_ref[o, 1], 0, both + LANES)
        hf = project(w3t_ref[o, 0], both, width)
        mixed = jnp.where(fwd[:, both:both + LANES], hf[:, :LANES], hb[:, both:])
        taps = jnp.concatenate([hb[:, :both], mixed, hf[:, LANES:]], axis=1)
        k = jnp.where(valid, taps * decay, 0.0)
        word = _bf16_bits(k) | (_bf16_bits(pltpu.roll(k, 1, 1)) << 16)
        for s in range(slabs):
            o_ref[o, pl.ds(s, rows, stride=pitch), :] = word[:, s * LANES:(s + 1) * LANES]
        for s in range(slabs, pitch):
            o_ref[o, pl.ds(s, rows, stride=pitch), :] = jnp.zeros((rows, LANES), jnp.uint32)


def _filters(seq_len, w1, b1, freq, w2, b2, w3):
    cfg = _tiles(seq_len)
    rows, pitch, width = cfg["filt_rows"], cfg["long"]["pitch"], cfg["long"]["width"]
    w3t = w3.T.reshape(HYENA_ORDER, N_DIRS, D_HYENA, FILTER_HIDDEN)
    col = lambda v: v.reshape(FILTER_HIDDEN, 1)
    full = lambda shape: pl.BlockSpec(shape, lambda i: (0,) * len(shape))
    return pl.pallas_call(
        functools.partial(_filter_kernel, seq_len=seq_len, cfg=cfg),
        grid=(D_HYENA // rows,),
        in_specs=[full((FILTER_HIDDEN, FILTER_EMB)), full((FILTER_HIDDEN, 1)), full((FILTER_HIDDEN, 1)),
                  full((FILTER_HIDDEN, FILTER_HIDDEN)), full((FILTER_HIDDEN, 1)),
                  pl.BlockSpec((HYENA_ORDER, N_DIRS, rows, FILTER_HIDDEN), lambda i: (0, 0, i, 0))],
        out_specs=pl.BlockSpec((HYENA_ORDER, rows * pitch, LANES), lambda i: (0, i, 0)),
        out_shape=jax.ShapeDtypeStruct((HYENA_ORDER, D_HYENA * pitch, LANES), jnp.uint32),
        scratch_shapes=[pltpu.VMEM((FILTER_HIDDEN, width), BF16), pltpu.VMEM((FILTER_HIDDEN, width), BF16)],
        compiler_params=_cparams(("arbitrary",)),
        name=f"hyena_filters_{seq_len}",
    )(w1.T, col(b1), col(freq), w2.T, col(b2), w3t)


def _conv_filter_words(dww):
    geo = _tiles(CONV_SPAN)["conv"]
    pad = CONV_KERNEL // 2
    tap = pad - (jnp.arange(geo["width"]) - geo["centre"])
    k = jnp.where((tap >= 0) & (tap < CONV_KERNEL), dww.T[:, jnp.clip(tap, 0, CONV_KERNEL - 1)], 0.0)
    bits = lambda v: lax.bitcast_convert_type(v.astype(BF16), jnp.uint16).astype(jnp.uint32)
    word = bits(k) | (bits(jnp.roll(k, 1, axis=1)) << 16)
    word = word.reshape(D_CONV, geo["slabs"], LANES)
    word = jnp.pad(word, ((0, 0), (0, geo["pitch"] - geo["slabs"]), (0, 0)))
    return word.reshape(1, D_CONV * geo["pitch"], LANES)


def _stage_a_kernel(x_ref, mod_ref, wct_ref, wg_ref, sga_ref, sgb_ref, at_ref, hyt_ref):
    shift1 = mod_ref[0, 0:1, :]
    scale1 = mod_ref[0, 1:2, :]
    half = x_ref.shape[1] // 2
    hs = [(_ln_rows(x_ref[0, r:r + half, :]) * (1.0 + scale1) + shift1).astype(BF16) for r in (0, half)]
    for i, h in enumerate(hs):
        rows = slice(i * half, (i + 1) * half)
        ct = lax.dot_general(wct_ref[...], h, NT_DIMS, preferred_element_type=F32)
        at_ref[0, :, rows] = ct[:D_CONV] * jax.nn.sigmoid(ct[D_CONV:2 * D_CONV])
        hyt_ref[0, :, rows] = ct[2 * D_CONV:]
        for c in range(2):
            cols = slice(c * D_CONV, (c + 1) * D_CONV)
            sga_ref[0, rows, cols] = jax.nn.sigmoid(jnp.dot(h, wg_ref[:, cols], preferred_element_type=F32))
            colsb = slice(D_MODEL + c * D_CONV, D_MODEL + (c + 1) * D_CONV)
            sgb_ref[0, rows, cols] = jax.nn.sigmoid(jnp.dot(h, wg_ref[:, colsb], preferred_element_type=F32))


def _const_spec(shape):
    return pl.BlockSpec(shape, lambda b, i: (0,) * len(shape), pipeline_mode=pl.Buffered(1))


def _stage_a(x, mod, wct, wg):
    bsz, seq_len, _ = x.shape
    tt = _tiles(seq_len)["tt_a"]
    hy_rows = (HYENA_ORDER + 1) * D_HYENA
    tok = pl.BlockSpec((1, tt, D_MODEL), lambda b, i: (b, i, 0))
    return pl.pallas_call(
        _stage_a_kernel,
        grid=(bsz, seq_len // tt),
        in_specs=[tok,
                  pl.BlockSpec((1, N_MOD, D_MODEL), lambda b, i: (b, 0, 0)),
                  _const_spec((2 * D_CONV + hy_rows, D_MODEL)),
                  _const_spec((D_MODEL, 2 * D_MODEL))],
        out_specs=[tok, tok,
                   pl.BlockSpec((1, D_CONV, tt), lambda b, i: (b, 0, i)),
                   pl.BlockSpec((1, hy_rows, tt), lambda b, i: (b, 0, i))],
        out_shape=[jax.ShapeDtypeStruct((bsz, seq_len, D_MODEL), F32),
                   jax.ShapeDtypeStruct((bsz, seq_len, D_MODEL), F32),
                   jax.ShapeDtypeStruct((bsz, D_CONV, seq_len), F32),
                   jax.ShapeDtypeStruct((bsz, hy_rows, seq_len), F32)],
        compiler_params=_cparams(("parallel", "parallel")),
        name=f"stage_a_{seq_len}",
    )(x, mod, wct, wg)


def _stage_b_kernel(sw_ref, sb_ref, skip_ref, dwb_ref, filt_ref, cfilt_ref, at_hbm, hyt_hbm, zb_hbm, cv_hbm,
                    inbuf, outbuf, conv_ref, toep_a, toep_b, toep_c, toep_d, ctoep_a, ctoep_b, ctoep_c, ctoep_d,
                    lhs_ref, clhs_ref, zf_ref, acc_ref, in_sem, out_sem, *, bsz, seq_len, cfg):
    nb, chans = cfg["nb"], cfg["ch_b"]
    geo, cgeo = cfg["long"], cfg["conv"]
    hy_parts = HYENA_ORDER + 1
    parts = hy_parts + 1
    pairs = chans // 2
    c_base = pl.program_id(0) * chans

    def in_copy(j, slot, part):
        c = c_base + j
        src = at_hbm.at[:, c, :] if part == hy_parts else hyt_hbm.at[:, part * D_HYENA + c, :]
        return pltpu.make_async_copy(src, inbuf.at[slot, part], in_sem.at[slot, part])

    def out_copy(j, slot, kind):
        dst = (zb_hbm, cv_hbm)[kind].at[:, c_base + j, :]
        return pltpu.make_async_copy(outbuf.at[slot, kind], dst, out_sem.at[slot, kind])

    lane = lax.broadcasted_iota(jnp.int32, (bsz, seq_len), 1)
    half = LANES // 2
    word_row = lax.broadcasted_iota(jnp.int32, (half, LANES), 0)
    col = lax.broadcasted_iota(jnp.int32, (half, LANES), 1)
    low = col <= 2 * word_row

    def build_strip(dst_ref, src_ref, row0, k_lo, k_hi, n_tiles):
        def put(offset, row_half, big):
            q, within = divmod(offset, TOEP)
            if 1 <= q <= n_tiles:
                r0 = (q - 1) * TOEP + row_half * LANES
                dst_ref[r0:r0 + LANES, within:within + LANES] = big

        prev = None
        for k in range(k_lo, k_hi + 1):
            fb = jnp.broadcast_to(src_ref[pl.ds(row0 + k, 1), :], (half, LANES))
            cur = pltpu.roll(fb, 1, 1, stride=2, stride_axis=0)
            if prev is not None:
                big = pltpu.bitcast(jnp.where(low, prev, cur), BF16)
                put((k - 1) * LANES, 1, big)
                if k >= 2:
                    put((k - 2) * LANES, 0, big)
            prev = cur

    ck_lo = (cgeo["centre"] - CONV_KERNEL // 2) // LANES - 1
    ck_hi = (cgeo["centre"] + CONV_KERNEL // 2) // LANES + 1
    long_tiles = 2 * nb - 1
    conv_tiles = 2 * (CONV_SPAN // TOEP) - 1

    def build_strips(toep_ref, ctoep_ref, j):
        for o in range(HYENA_ORDER):
            build_strip(toep_ref.at[o], filt_ref.at[o], j * geo["pitch"], 0, geo["slabs"] - 1, long_tiles)
        build_strip(ctoep_ref, cfilt_ref.at[0], j * cgeo["pitch"], ck_lo, ck_hi, conv_tiles)

    def spread(dst_ref, h, n_tiles, block):
        reach = n_tiles // 2
        for si in range(nb):
            blk = block(si).astype(BF16)
            for q in range(n_tiles):
                tj = si + q - reach
                if 0 <= tj < nb:
                    dst_ref[h, tj * bsz:(tj + 1) * bsz, q * TOEP:(q + 1) * TOEP] = blk

    def apply(src_ref, h, stack):
        k = src_ref.shape[-1]
        ks = -(-(k // TOEP) // 2) * TOEP
        res = jnp.dot(src_ref[h, :, :ks], stack[:ks, :], preferred_element_type=F32)
        if ks < k:
            res = res + jnp.dot(src_ref[h, :, ks:], stack[ks:, :], preferred_element_type=F32)
        return res

    def long_conv_pair(toeps, o):
        def tile(h, d):
            return toeps[h][o, (nb - 1 + d) * TOEP:(nb + d) * TOEP, :]
        for h in range(2):
            acc_ref[h] = jnp.dot(lhs_ref[h], tile(h, 0), preferred_element_type=F32)
        for d in range(1, nb):
            keep = (nb - d) * bsz
            for h in range(2):
                acc_ref[h, d * bsz:, :] += jnp.dot(lhs_ref[h, :keep, :], tile(h, d), preferred_element_type=F32)
            for h in range(2):
                acc_ref[h, :keep, :] += jnp.dot(lhs_ref[h, d * bsz:, :], tile(h, -d), preferred_element_type=F32)

    def col(si):
        return slice(si * TOEP, (si + 1) * TOEP)

    def row(si):
        return slice(si * bsz, (si + 1) * bsz)

    def pair_compute(j0, in_slots, out_slots, toeps, ctoeps):
        cs = [c_base + j0 + h for h in range(2)]
        for h in range(2):
            spread(clhs_ref, h, conv_tiles, lambda si, h=h: inbuf[in_slots[h], hy_parts, :, col(si)])
        cvs = [apply(clhs_ref, h, ctoeps[h]) for h in range(2)]
        for h in range(2):
            for si in range(nb):
                outbuf[out_slots[h], 1, :, col(si)] = cvs[h][row(si), :] + dwb_ref[cs[h]]
            out_copy(j0 + h, out_slots[h], 1).start()

        for h in range(2):
            for part in range(hy_parts):
                p = inbuf[in_slots[h], part]
                ch = part * D_HYENA + cs[h]
                before = jnp.where(lane == 0, 0.0, pltpu.roll(p, 1, 1))
                after = jnp.where(lane == seq_len - 1, 0.0, pltpu.roll(p, seq_len - 1, 1))
                conv_ref[h, part] = (sw_ref[ch] * before + sw_ref[hy_parts * D_HYENA + ch] * p
                                     + sw_ref[2 * hy_parts * D_HYENA + ch] * after + sb_ref[ch])
            for si in range(nb):
                lhs_ref[h, row(si), :] = conv_ref[h, 0, :, col(si)].astype(BF16)
        long_conv_pair(toeps, 0)
        for h in range(2):
            skip0 = skip_ref[cs[h]]
            for si in range(nb):
                z = conv_ref[h, 1, :, col(si)] * (acc_ref[h, row(si), :] + conv_ref[h, 0, :, col(si)] * skip0)
                zf_ref[h, row(si), :] = z
                lhs_ref[h, row(si), :] = z.astype(BF16)
        long_conv_pair(toeps, 1)
        for h in range(2):
            skip1 = skip_ref[D_HYENA + cs[h]]
            for si in range(nb):
                outbuf[out_slots[h], 0, :, col(si)] = (conv_ref[h, 2, :, col(si)]
                                                       * (acc_ref[h, row(si), :] + zf_ref[h, row(si), :] * skip1))
            out_copy(j0 + h, out_slots[h], 0).start()

    def in_slots_of(p):
        base = 2 * lax.rem(p, 3)
        return (base, base + 1)

    def pair_stage(p, j0, out_slots, toeps, ctoeps, next_toeps, next_ctoeps, have_prev):
        slots = in_slots_of(p)
        for h in range(2):
            for part in range(parts):
                in_copy(j0 + h, slots[h], part).wait()

        @pl.when(p + 2 < pairs)
        def _():
            ahead = in_slots_of(p + 2)
            for h in range(2):
                for part in range(parts):
                    in_copy(j0 + 4 + h, ahead[h], part).start()

        @pl.when(have_prev)
        def _():
            for h in range(2):
                for kind in range(2):
                    out_copy(j0 - 4 + h, out_slots[h], kind).wait()

        for h in range(2):
            build_strips(next_toeps[h], next_ctoeps[h], jnp.minimum(j0 + 2 + h, chans - 1))
        pair_compute(j0, slots, out_slots, toeps, ctoeps)

    toeps_even, toeps_odd = (toep_a, toep_b), (toep_c, toep_d)
    ctoeps_even, ctoeps_odd = (ctoep_a, ctoep_b), (ctoep_c, ctoep_d)

    def quad(i, carry):
        pair_stage(2 * i, 4 * i, (0, 1), toeps_even, ctoeps_even, toeps_odd, ctoeps_odd, i >= 1)
        pair_stage(2 * i + 1, 4 * i + 2, (2, 3), toeps_odd, ctoeps_odd, toeps_even, ctoeps_even, i >= 1)
        return carry

    for p in range(2):
        for h in range(2):
            for part in range(parts):
                in_copy(2 * p + h, 2 * p + h, part).start()
    for ref in (ctoep_a, ctoep_b, ctoep_c, ctoep_d, clhs_ref):
        ref[...] = jnp.zeros(ref.shape, BF16)
    for h in range(2):
        build_strips(toeps_even[h], ctoeps_even[h], h)
    lax.fori_loop(0, pairs // 2, quad, 0)
    for p in range(pairs - 2, pairs):
        for h in range(2):
            for kind in range(2):
                out_copy(2 * p + h, 2 * (p % 2) + h, kind).wait()


def _stage_b(at, hyt, filt, cfilt, short_w, short_b, skip, dwb):
    bsz, _, seq_len = hyt.shape
    cfg = _tiles(seq_len)
    chans, nb = cfg["ch_b"], cfg["nb"]
    parts = HYENA_ORDER + 2
    long_k = (2 * nb - 1) * TOEP
    conv_k = (2 * (CONV_SPAN // TOEP) - 1) * TOEP
    smem = pl.BlockSpec(memory_space=pltpu.SMEM)
    hbm = pl.BlockSpec(memory_space=pl.ANY)
    chan_major = jax.ShapeDtypeStruct((bsz, D_HYENA, seq_len), F32)
    return pl.pallas_call(
        functools.partial(_stage_b_kernel, bsz=bsz, seq_len=seq_len, cfg=cfg),
        grid=(D_HYENA // chans,),
        in_specs=[smem, smem, smem, smem,
                  pl.BlockSpec((HYENA_ORDER, chans * cfg["long"]["pitch"], LANES), lambda g: (0, g, 0)),
                  pl.BlockSpec((1, chans * cfg["conv"]["pitch"], LANES), lambda g: (0, g, 0)),
                  hbm, hbm],
        out_specs=[hbm, hbm],
        out_shape=[chan_major, chan_major],
        scratch_shapes=[
            pltpu.VMEM((6, parts, bsz, seq_len), F32),
            pltpu.VMEM((4, 2, bsz, seq_len), F32),
            pltpu.VMEM((2, HYENA_ORDER + 1, bsz, seq_len), F32),
            *[pltpu.VMEM((HYENA_ORDER, long_k, TOEP), BF16)] * 4,
            *[pltpu.VMEM((conv_k, TOEP), BF16)] * 4,
            pltpu.VMEM((2, nb * bsz, TOEP), BF16),
            pltpu.VMEM((2, nb * bsz, conv_k), BF16),
            pltpu.VMEM((2, nb * bsz, TOEP), F32),
            pltpu.VMEM((2, nb * bsz, TOEP), F32),
            pltpu.SemaphoreType.DMA((6, parts)),
            pltpu.SemaphoreType.DMA((4, 2)),
        ],
        compiler_params=_cparams(("arbitrary",)),
        name=f"stage_b_{seq_len}",
    )(short_w.reshape(-1), short_b.reshape(-1), skip.reshape(-1), dwb.reshape(-1), filt, cfilt, at, hyt)


def _stage_c_kernel(x_ref, sga_ref, sgb_ref, zbt_ref, cvt_ref, mod_ref, lng_ref, lnb_ref, wpw_ref, whyo_ref,
                    wout_ref, ln1g_ref, ln1b_ref, w1_ref, b1_ref, w2_ref, b2_ref, ln2g_ref, ln2b_ref, o_ref,
                    *, ff_chunk):
    gate1 = mod_ref[0, 2:3, :]
    shift2 = mod_ref[0, 3:4, :]
    scale2 = mod_ref[0, 4:5, :]
    gate2 = mod_ref[0, 5:6, :]
    half = x_ref.shape[1] // 2

    def mixer(rows):
        a2 = _ln_rows(cvt_ref[0, :, rows].T) * lng_ref[...] + lnb_ref[...]
        a2 = (a2 * jax.nn.sigmoid(a2)).astype(BF16)
        ya = jnp.dot(a2, wpw_ref[...], preferred_element_type=F32)
        zb = zbt_ref[0, :, rows].T.astype(BF16)
        yb = jnp.dot(zb, whyo_ref[...], preferred_element_type=F32)
        m = sga_ref[0, rows, :] * ya + sgb_ref[0, rows, :] * yb
        mix = jnp.dot(m.astype(BF16), wout_ref[...], preferred_element_type=F32)
        x1 = _ln_rows(DEEPNORM_ALPHA * x_ref[0, rows, :] + gate1 * mix) * ln1g_ref[...] + ln1b_ref[...]
        return x1, (_ln_rows(x1) * (1.0 + scale2) + shift2).astype(BF16)

    def mlp(x1, h2, rows):
        f = jnp.zeros(x1.shape, F32) + b2_ref[...]
        for j in range(D_FF // ff_chunk):
            cols = slice(j * ff_chunk, (j + 1) * ff_chunk)
            u = jnp.maximum(jnp.dot(h2, w1_ref[:, cols], preferred_element_type=F32) + b1_ref[:, cols], 0.0)
            f = f + jnp.dot((u * u).astype(BF16), w2_ref[cols, :], preferred_element_type=F32)
        o_ref[0, rows, :] = _ln_rows(DEEPNORM_ALPHA * x1 + gate2 * f) * ln2g_ref[...] + ln2b_ref[...]

    rows_a, rows_b = slice(0, half), slice(half, 2 * half)
    x1_a, h2_a = mixer(rows_a)
    x1_b, h2_b = mixer(rows_b)
    mlp(x1_a, h2_a, rows_a)
    mlp(x1_b, h2_b, rows_b)


def _stage_c(x, sga, sgb, zbt, cvt, mod, lng, lnb, wpw, whyo, wout, ln1g, ln1b, w1, b1, w2, b2, ln2g, ln2b):
    bsz, seq_len, _ = x.shape
    tt = _tiles(seq_len)["tt_c"]
    tok = pl.BlockSpec((1, tt, D_MODEL), lambda b, i: (b, i, 0))
    chan = pl.BlockSpec((1, D_HYENA, tt), lambda b, i: (b, 0, i))
    return pl.pallas_call(
        functools.partial(_stage_c_kernel, ff_chunk=D_MODEL),
        grid=(bsz, seq_len // tt),
        in_specs=[tok, tok, tok, chan, chan,
                  pl.BlockSpec((1, N_MOD, D_MODEL), lambda b, i: (b, 0, 0)),
                  _const_spec((1, D_CONV)), _const_spec((1, D_CONV)), _const_spec((D_CONV, D_MODEL)),
                  _const_spec((D_HYENA, D_MODEL)), _const_spec((D_MODEL, D_MODEL)),
                  _const_spec((1, D_MODEL)), _const_spec((1, D_MODEL)),
                  _const_spec((D_MODEL, D_FF)), _const_spec((1, D_FF)),
                  _const_spec((D_FF, D_MODEL)), _const_spec((1, D_MODEL)),
                  _const_spec((1, D_MODEL)), _const_spec((1, D_MODEL))],
        out_specs=tok,
        out_shape=jax.ShapeDtypeStruct((bsz, seq_len, D_MODEL), F32),
        compiler_params=_cparams(("parallel", "parallel")),
        name=f"stage_c_{seq_len}",
    )(x, sga, sgb, zbt, cvt, mod, lng, lnb, wpw, whyo, wout, ln1g, ln1b, w1, b1, w2, b2, ln2g, ln2b)


def _encoder(x, mod, filt, p):
    sga, sgb, at, hyt = _stage_a(x, mod, p["wct"], p["wg"])
    zbt, cvt = _stage_b(at, hyt, filt, p["cfilt"], p["short_w"], p["short_b"], p["skip"], p["dwb"])
    return _stage_c(x, sga, sgb, zbt, cvt, mod, p["lng"], p["lnb"], p["wpw"], p["whyo"], p["wout"],
                    p["ln1g"], p["ln1b"], p["w1"], p["b1"], p["w2"], p["b2"], p["ln2g"], p["ln2b"])


def _prepare(w_in, conv_dw_w, conv_dw_b, conv_ln_g, conv_ln_b, conv_pw_w, hy_short_w, hy_short_b, hy_skip,
             hy_out_w, w_out, ln1_g, ln1_b, mlp_w1, mlp_b1, mlp_w2, mlp_b2, ln2_g, ln2_b, l=0):
    row = lambda v: v.reshape(1, -1)
    s2 = 2 * D_CONV + (HYENA_ORDER + 1) * D_HYENA
    w_in_l = w_in[l]
    return dict(
        wct=w_in_l[:, :s2].T.astype(BF16),
        wg=w_in_l[:, s2:].astype(BF16),
        cfilt=_conv_filter_words(conv_dw_w[l]), dwb=conv_dw_b[l],
        lng=row(conv_ln_g[l]), lnb=row(conv_ln_b[l]), wpw=conv_pw_w[l].astype(BF16),
        short_w=hy_short_w[l], short_b=hy_short_b[l], skip=hy_skip[l],
        whyo=hy_out_w[l].astype(BF16), wout=w_out[l].astype(BF16),
        ln1g=row(ln1_g[l]), ln1b=row(ln1_b[l]),
        w1=mlp_w1[l].astype(BF16), b1=row(mlp_b1[l]), w2=mlp_w2[l].astype(BF16), b2=row(mlp_b2[l]),
        ln2g=row(ln2_g[l]), ln2b=row(ln2_b[l]),
    )


def kernel(x_prompt, x_sample, c_prompt, c_sample, w_ada, b_ada, w_in, conv_dw_w, conv_dw_b, conv_ln_g, conv_ln_b, conv_pw_w, hy_short_w, hy_short_b, hy_ffn_w1, hy_ffn_b1, hy_sin_freq, hy_ffn_w2, hy_ffn_b2, hy_ffn_w3, hy_skip, hy_out_w, w_out, ln1_g, ln1_b, mlp_w1, mlp_b1, mlp_w2, mlp_b2, ln2_g, ln2_b):
    l = 0
    p = _prepare(w_in, conv_dw_w, conv_dw_b, conv_ln_g, conv_ln_b, conv_pw_w, hy_short_w, hy_short_b, hy_skip,
                 hy_out_w, w_out, ln1_g, ln1_b, mlp_w1, mlp_b1, mlp_w2, mlp_b2, ln2_g, ln2_b, l)
    nb_p = x_prompt.shape[0]
    mod = _mod(jnp.concatenate([c_prompt, c_sample], axis=0), w_ada[l], b_ada[l])
    mod = mod.reshape(-1, N_MOD, D_MODEL)
    outs = []
    for x, m in ((x_prompt, mod[:nb_p]), (x_sample, mod[nb_p:])):
        filt = _filters(x.shape[1], hy_ffn_w1[l], hy_ffn_b1[l], hy_sin_freq[l],
                        hy_ffn_w2[l], hy_ffn_b2[l], hy_ffn_w3[l])
        outs.append(_encoder(x, m, filt, p))
    return tuple(outs)
```

```python
import functools
import math

import jax
import jax.numpy as jnp
from jax import lax
from jax.experimental import pallas as pl
from jax.experimental.pallas import tpu as pltpu

F32 = jnp.float32
BF16 = jnp.bfloat16
HIGHEST = lax.Precision.HIGHEST

LANES = 128
SUBLANES = 8
MXU_DIM = 256
VMEM_LIMIT_BYTES = 56 * 1024 * 1024

D_MODEL = 1024
D_CONV = D_MODEL // 2
D_HYENA = D_MODEL // 2
CONV_KERNEL = 31
SHORT_KERNEL = 3
HYENA_ORDER = 2
N_BANDS = 16
FILTER_EMB = 1 + 2 * N_BANDS
FILTER_HIDDEN = 64
N_DIRS = 2
D_FF = 4 * D_MODEL
N_MOD = 6
LN_EPS = 1e-5
DEPTH = 1
MAX_DECAY = math.log(1e-2) / 0.3
MIN_DECAY = math.log(1e-2) / 1.5
DEEPNORM_ALPHA = (2.0 * DEPTH) ** 0.25

TOEP = MXU_DIM
CONV_SPAN = 2 * TOEP
NT_DIMS = (((1,), (1,)), ((), ()))


def _strip_geometry(span):
    width = 2 * span + TOEP
    slabs = width // LANES
    return dict(width=width, slabs=slabs, pitch=-(-slabs // SUBLANES) * SUBLANES, centre=span + TOEP - 1)


def _tiles(seq_len):
    return dict(
        tt_a=512,
        tt_c=512,
        ch_b=32,
        filt_rows=128,
        nb=seq_len // TOEP,
        long=_strip_geometry(seq_len),
        conv=_strip_geometry(CONV_SPAN),
    )


def _cparams(sem):
    return pltpu.CompilerParams(dimension_semantics=sem, vmem_limit_bytes=VMEM_LIMIT_BYTES)


def _ln_rows(x):
    mu = jnp.mean(x, axis=-1, keepdims=True)
    xc = x - mu
    var = jnp.mean(xc * xc, axis=-1, keepdims=True)
    return xc * lax.rsqrt(var + LN_EPS)


def _mod_kernel(c_ref, w_ref, b_ref, o_ref):
    c = c_ref[...]
    s = c * jax.nn.sigmoid(c)
    o_ref[...] = jnp.dot(s, w_ref[...], precision=HIGHEST, preferred_element_type=F32) + b_ref[...]


def _mod(c_all, w_ada, b_ada):
    rows = c_all.shape[0]
    n = w_ada.shape[1]
    bn = D_MODEL
    return pl.pallas_call(
        _mod_kernel,
        grid=(n // bn,),
        in_specs=[pl.BlockSpec((rows, D_MODEL), lambda j: (0, 0)),
                  pl.BlockSpec((D_MODEL, bn), lambda j: (0, j)),
                  pl.BlockSpec((1, bn), lambda j: (0, j))],
        out_specs=pl.BlockSpec((rows, bn), lambda j: (0, j)),
        out_shape=jax.ShapeDtypeStruct((rows, n), F32),
        compiler_params=_cparams(("parallel",)),
        name="adaln_mod",
    )(c_all, w_ada, b_ada.reshape(1, n))


def _bf16_bits(x):
    b = lax.bitcast_convert_type(x, jnp.uint32)
    return (b + jnp.uint32(0x7FFF) + ((b >> 16) & jnp.uint32(1))) >> 16


def _split_bf16(x):
    hi = x.astype(BF16)
    return hi, (x - hi.astype(F32)).astype(BF16)


def _filter_kernel(w1t_ref, b1_ref, fr_ref, w2t_ref, b2_ref, w3t_ref, o_ref, hdn_hi_ref, hdn_lo_ref,
                   *, seq_len, cfg):
    geo, rows = cfg["long"], cfg["filt_rows"]
    width, slabs, pitch, centre = geo["width"], geo["slabs"], geo["pitch"], geo["centre"]
    m = lax.broadcasted_iota(jnp.int32, (1, width), 1)
    tau = jnp.abs(m - centre).astype(F32)
    t = tau * (1.0 / (seq_len - 1))

    @pl.when(pl.program_id(0) == 0)
    def _():
        w = tau * (2.0 * math.pi / seq_len)
        bidx = lax.broadcasted_iota(jnp.int32, (N_BANDS, 1), 0).astype(F32)
        bands = 1e-4 + bidx * ((N_BANDS - 1 - 1e-4) / (N_BANDS - 1))
        ang = bands * w
        z = jnp.concatenate([t, jnp.cos(ang), -jnp.sin(ang)], axis=0)
        fr = fr_ref[...]
        h1 = jnp.sin(fr * (jnp.dot(w1t_ref[...], z, precision=HIGHEST, preferred_element_type=F32) + b1_ref[...]))
        hdn = jnp.sin(fr * (jnp.dot(w2t_ref[...], h1, precision=HIGHEST, preferred_element_type=F32) + b2_ref[...]))
        hdn_hi_ref[...], hdn_lo_ref[...] = _split_bf16(hdn)

    c0 = pl.program_id(0) * rows
    cidx = (lax.broadcasted_iota(jnp.int32, (rows, 1), 0) + c0).astype(F32)
    step = (MAX_DECAY - MIN_DECAY) / (D_HYENA - 1)
    delta_f = jnp.abs(MIN_DECAY + cidx * step)
    delta_b = jnp.abs(MIN_DECAY + (D_HYENA - 1 - cidx) * step)
    fwd = m >= centre
    decay = jnp.exp(-t * jnp.where(fwd, delta_f, delta_b))
    valid = tau <= float(seq_len - 1)
    both = centre // LANES * LANES

    def project(wt, lo, hi):
        w_hi, w_lo = _split_bf16(wt)
        h_hi, h_lo = hdn_hi_ref[:, lo:hi], hdn_lo_ref[:, lo:hi]
        dot = functools.partial(jnp.dot, preferred_element_type=F32)
        return dot(w_hi, h_hi) + dot(w_hi, h_lo) + dot(w_lo, h_hi)

    for o in range(HYENA_ORDER):
        hb = project(w3t_ref[o, 1], 0, both + LANES)
        hf = project(w3t_ref[o, 0], both, width)
        mixed = jnp.where(fwd[:, both:both + LANES], hf[:, :LANES], hb[:, both:])
        taps = jnp.concatenate([hb[:, :both], mixed, hf[:, LANES:]], axis=1)
        k = jnp.where(valid, taps * decay, 0.0)
        word = _bf16_bits(k) | (_bf16_bits(pltpu.roll(k, 1, 1)) << 16)
        for s in range(slabs):
            o_ref[o, pl.ds(s, rows, stride=pitch), :] = word[:, s * LANES:(s + 1) * LANES]
        for s in range(slabs, pitch):
            o_ref[o, pl.ds(s, rows, stride=pitch), :] = jnp.zeros((rows, LANES), jnp.uint32)


def _filters(seq_len, w1, b1, freq, w2, b2, w3):
    cfg = _tiles(seq_len)
    rows, pitch, width = cfg["filt_rows"], cfg["long"]["pitch"], cfg["long"]["width"]
    w3t = w3.T.reshape(HYENA_ORDER, N_DIRS, D_HYENA, FILTER_HIDDEN)
    col = lambda v: v.reshape(FILTER_HIDDEN, 1)
    full = lambda shape: pl.BlockSpec(shape, lambda i: (0,) * len(shape))
    return pl.pallas_call(
        functools.partial(_filter_kernel, seq_len=seq_len, cfg=cfg),
        grid=(D_HYENA // rows,),
        in_specs=[full((FILTER_HIDDEN, FILTER_EMB)), full((FILTER_HIDDEN, 1)), full((FILTER_HIDDEN, 1)),
                  full((FILTER_HIDDEN, FILTER_HIDDEN)), full((FILTER_HIDDEN, 1)),
                  pl.BlockSpec((HYENA_ORDER, N_DIRS, rows, FILTER_HIDDEN), lambda i: (0, 0, i, 0))],
        out_specs=pl.BlockSpec((HYENA_ORDER, rows * pitch, LANES), lambda i: (0, i, 0)),
        out_shape=jax.ShapeDtypeStruct((HYENA_ORDER, D_HYENA * pitch, LANES), jnp.uint32),
        scratch_shapes=[pltpu.VMEM((FILTER_HIDDEN, width), BF16), pltpu.VMEM((FILTER_HIDDEN, width), BF16)],
        compiler_params=_cparams(("arbitrary",)),
        name=f"hyena_filters_{seq_len}",
    )(w1.T, col(b1), col(freq), w2.T, col(b2), w3t)


def _conv_filter_words(dww):
    geo = _tiles(CONV_SPAN)["conv"]
    pad = CONV_KERNEL // 2
    tap = pad - (jnp.arange(geo["width"]) - geo["centre"])
    k = jnp.where((tap >= 0) & (tap < CONV_KERNEL), dww.T[:, jnp.clip(tap, 0, CONV_KERNEL - 1)], 0.0)
    bits = lambda v: lax.bitcast_convert_type(v.astype(BF16), jnp.uint16).astype(jnp.uint32)
    word = bits(k) | (bits(jnp.roll(k, 1, axis=1)) << 16)
    word = word.reshape(D_CONV, geo["slabs"], LANES)
    word = jnp.pad(word, ((0, 0), (0, geo["pitch"] - geo["slabs"]), (0, 0)))
    return word.reshape(1, D_CONV * geo["pitch"], LANES)


def _stage_a_kernel(x_ref, mod_ref, wct_ref, wg_ref, sga_ref, sgb_ref, at_ref, hyt_ref):
    shift1 = mod_ref[0, 0:1, :]
    scale1 = mod_ref[0, 1:2, :]
    half = x_ref.shape[1] // 2
    hs = [(_ln_rows(x_ref[0, r:r + half, :]) * (1.0 + scale1) + shift1).astype(BF16) for r in (0, half)]
    for i, h in enumerate(hs):
        rows = slice(i * half, (i + 1) * half)
        ct = lax.dot_general(wct_ref[...], h, NT_DIMS, preferred_element_type=F32)
        at_ref[0, :, rows] = ct[:D_CONV] * jax.nn.sigmoid(ct[D_CONV:2 * D_CONV])
        hyt_ref[0, :, rows] = ct[2 * D_CONV:]
        for c in range(2):
            cols = slice(c * D_CONV, (c + 1) * D_CONV)
            sga_ref[0, rows, cols] = jax.nn.sigmoid(jnp.dot(h, wg_ref[:, cols], preferred_element_type=F32))
            colsb = slice(D_MODEL + c * D_CONV, D_MODEL + (c + 1) * D_CONV)
            sgb_ref[0, rows, cols] = jax.nn.sigmoid(jnp.dot(h, wg_ref[:, colsb], preferred_element_type=F32))


def _const_spec(shape):
    return pl.BlockSpec(shape, lambda b, i: (0,) * len(shape), pipeline_mode=pl.Buffered(1))


def _stage_a(x, mod, wct, wg):
    bsz, seq_len, _ = x.shape
    tt = _tiles(seq_len)["tt_a"]
    hy_rows = (HYENA_ORDER + 1) * D_HYENA
    tok = pl.BlockSpec((1, tt, D_MODEL), lambda b, i: (b, i, 0))
    return pl.pallas_call(
        _stage_a_kernel,
        grid=(bsz, seq_len // tt),
        in_specs=[tok,
                  pl.BlockSpec((1, N_MOD, D_MODEL), lambda b, i: (b, 0, 0)),
                  _const_spec((2 * D_CONV + hy_rows, D_MODEL)),
                  _const_spec((D_MODEL, 2 * D_MODEL))],
        out_specs=[tok, tok,
                   pl.BlockSpec((1, D_CONV, tt), lambda b, i: (b, 0, i)),
                   pl.BlockSpec((1, hy_rows, tt), lambda b, i: (b, 0, i))],
        out_shape=[jax.ShapeDtypeStruct((bsz, seq_len, D_MODEL), F32),
                   jax.ShapeDtypeStruct((bsz, seq_len, D_MODEL), F32),
                   jax.ShapeDtypeStruct((bsz, D_CONV, seq_len), F32),
                   jax.ShapeDtypeStruct((bsz, hy_rows, seq_len), F32)],
        compiler_params=_cparams(("parallel", "parallel")),
        name=f"stage_a_{seq_len}",
    )(x, mod, wct, wg)


def _stage_b_kernel(sw_ref, sb_ref, skip_ref, dwb_ref, filt_ref, cfilt_ref, at_hbm, hyt_hbm, zb_hbm, cv_hbm,
                    inbuf, outbuf, conv_ref, toep_a, toep_b, toep_c, toep_d, ctoep_a, ctoep_b, ctoep_c, ctoep_d,
                    lhs_ref, clhs_ref, zf_ref, acc_ref, in_sem, out_sem, *, bsz, seq_len, cfg):
    nb, chans = cfg["nb"], cfg["ch_b"]
    geo, cgeo = cfg["long"], cfg["conv"]
    hy_parts = HYENA_ORDER + 1
    parts = hy_parts + 1
    pairs = chans // 2
    c_base = pl.program_id(0) * chans

    def in_copy(j, slot, part):
        c = c_base + j
        src = at_hbm.at[:, c, :] if part == hy_parts else hyt_hbm.at[:, part * D_HYENA + c, :]
        return pltpu.make_async_copy(src, inbuf.at[slot, part], in_sem.at[slot, part])

    def out_copy(j, slot, kind):
        dst = (zb_hbm, cv_hbm)[kind].at[:, c_base + j, :]
        return pltpu.make_async_copy(outbuf.at[slot, kind], dst, out_sem.at[slot, kind])

    lane = lax.broadcasted_iota(jnp.int32, (bsz, seq_len), 1)
    half = LANES // 2
    word_row = lax.broadcasted_iota(jnp.int32, (half, LANES), 0)
    col = lax.broadcasted_iota(jnp.int32, (half, LANES), 1)
    low = col <= 2 * word_row

    def build_strip(dst_ref, src_ref, row0, k_lo, k_hi, n_tiles):
        def put(offset, row_half, big):
            q, within = divmod(offset, TOEP)
            if 1 <= q <= n_tiles:
                r0 = (q - 1) * TOEP + row_half * LANES
                dst_ref[r0:r0 + LANES, within:within + LANES] = big

        prev = None
        for k in range(k_lo, k_hi + 1):
            fb = jnp.broadcast_to(src_ref[pl.ds(row0 + k, 1), :], (half, LANES))
            cur = pltpu.roll(fb, 1, 1, stride=2, stride_axis=0)
            if prev is not None:
                big = pltpu.bitcast(jnp.where(low, prev, cur), BF16)
                put((k - 1) * LANES, 1, big)
                if k >= 2:
                    put((k - 2) * LANES, 0, big)
            prev = cur

    ck_lo = (cgeo["centre"] - CONV_KERNEL // 2) // LANES - 1
    ck_hi = (cgeo["centre"] + CONV_KERNEL // 2) // LANES + 1
    long_tiles = 2 * nb - 1
    conv_tiles = 2 * (CONV_SPAN // TOEP) - 1

    def build_strips(toep_ref, ctoep_ref, j):
        for o in range(HYENA_ORDER):
            build_strip(toep_ref.at[o], filt_ref.at[o], j * geo["pitch"], 0, geo["slabs"] - 1, long_tiles)
        build_strip(ctoep_ref, cfilt_ref.at[0], j * cgeo["pitch"], ck_lo, ck_hi, conv_tiles)

    def spread(dst_ref, h, n_tiles, block):
        reach = n_tiles // 2
        for si in range(nb):
            blk = block(si).astype(BF16)
            for q in range(n_tiles):
                tj = si + q - reach
                if 0 <= tj < nb:
                    dst_ref[h, tj * bsz:(tj + 1) * bsz, q * TOEP:(q + 1) * TOEP] = blk

    def apply(src_ref, h, stack):
        k = src_ref.shape[-1]
        ks = -(-(k // TOEP) // 2) * TOEP
        res = jnp.dot(src_ref[h, :, :ks], stack[:ks, :], preferred_element_type=F32)
        if ks < k:
            res = res + jnp.dot(src_ref[h, :, ks:], stack[ks:, :], preferred_element_type=F32)
        return res

    def long_conv_pair(toeps, o):
        def tile(h, d):
            return toeps[h][o, (nb - 1 + d) * TOEP:(nb + d) * TOEP, :]
        for h in range(2):
            acc_ref[h] = jnp.dot(lhs_ref[h], tile(h, 0), preferred_element_type=F32)
        for d in range(1, nb):
            keep = (nb - d) * bsz
            for h in range(2):
                acc_ref[h, d * bsz:, :] += jnp.dot(lhs_ref[h, :keep, :], tile(h, d), preferred_element_type=F32)
            for h in range(2):
                acc_ref[h, :keep, :] += jnp.dot(lhs_ref[h, d * bsz:, :], tile(h, -d), preferred_element_type=F32)

    def col(si):
        return slice(si * TOEP, (si + 1) * TOEP)

    def row(si):
        return slice(si * bsz, (si + 1) * bsz)

    def pair_compute(j0, in_slots, out_slots, toeps, ctoeps):
        cs = [c_base + j0 + h for h in range(2)]
        for h in range(2):
            spread(clhs_ref, h, conv_tiles, lambda si, h=h: inbuf[in_slots[h], hy_parts, :, col(si)])
        cvs = [apply(clhs_ref, h, ctoeps[h]) for h in range(2)]
        for h in range(2):
            for si in range(nb):
                outbuf[out_slots[h], 1, :, col(si)] = cvs[h][row(si), :] + dwb_ref[cs[h]]

        for h in range(2):
            for part in range(hy_parts):
                p = inbuf[in_slots[h], part]
                ch = part * D_HYENA + cs[h]
                before = jnp.where(lane == 0, 0.0, pltpu.roll(p, 1, 1))
                after = jnp.where(lane == seq_len - 1, 0.0, pltpu.roll(p, seq_len - 1, 1))
                conv_ref[h, part] = (sw_ref[ch] * before + sw_ref[hy_parts * D_HYENA + ch] * p
                                     + sw_ref[2 * hy_parts * D_HYENA + ch] * after + sb_ref[ch])
            for si in range(nb):
                lhs_ref[h, row(si), :] = conv_ref[h, 0, :, col(si)].astype(BF16)
        long_conv_pair(toeps, 0)
        for h in range(2):
            skip0 = skip_ref[cs[h]]
            for si in range(nb):
                z = conv_ref[h, 1, :, col(si)] * (acc_ref[h, row(si), :] + conv_ref[h, 0, :, col(si)] * skip0)
                zf_ref[h, row(si), :] = z
                lhs_ref[h, row(si), :] = z.astype(BF16)
        long_conv_pair(toeps, 1)
        for h in range(2):
            skip1 = skip_ref[D_HYENA + cs[h]]
            for si in range(nb):
                outbuf[out_slots[h], 0, :, col(si)] = (conv_ref[h, 2, :, col(si)]
                                                       * (acc_ref[h, row(si), :] + zf_ref[h, row(si), :] * skip1))
        for h in range(2):
            for kind in range(2):
                out_copy(j0 + h, out_slots[h], kind).start()

    def in_slots_of(p):
        base = 2 * lax.rem(p, 3)
        return (base, base + 1)

    def pair_stage(p, j0, out_slots, toeps, ctoeps, next_toeps, next_ctoeps, have_prev):
        slots = in_slots_of(p)
        for h in range(2):
            for part in range(parts):
                in_copy(j0 + h, slots[h], part).wait()

        @pl.when(p + 2 < pairs)
        def _():
            ahead = in_slots_of(p + 2)
            for h in range(2):
                for part in range(parts):
                    in_copy(j0 + 4 + h, ahead[h], part).start()

        @pl.when(have_prev)
        def _():
            for h in range(2):
                for kind in range(2):
                    out_copy(j0 - 4 + h, out_slots[h], kind).wait()

        for h in range(2):
            build_strips(next_toeps[h], next_ctoeps[h], jnp.minimum(j0 + 2 + h, chans - 1))
        pair_compute(j0, slots, out_slots, toeps, ctoeps)

    toeps_even, toeps_odd = (toep_a, toep_b), (toep_c, toep_d)
    ctoeps_even, ctoeps_odd = (ctoep_a, ctoep_b), (ctoep_c, ctoep_d)

    def quad(i, carry):
        pair_stage(2 * i, 4 * i, (0, 1), toeps_even, ctoeps_even, toeps_odd, ctoeps_odd, i >= 1)
        pair_stage(2 * i + 1, 4 * i + 2, (2, 3), toeps_odd, ctoeps_odd, toeps_even, ctoeps_even, i >= 1)
        return carry

    for p in range(2):
        for h in range(2):
            for part in range(parts):
                in_copy(2 * p + h, 2 * p + h, part).start()
    for ref in (ctoep_a, ctoep_b, ctoep_c, ctoep_d, clhs_ref):
        ref[...] = jnp.zeros(ref.shape, BF16)
    for h in range(2):
        build_strips(toeps_even[h], ctoeps_even[h], h)
    lax.fori_loop(0, pairs // 2, quad, 0)
    for p in range(pairs - 2, pairs):
        for h in range(2):
            for kind in range(2):
                out_copy(2 * p + h, 2 * (p % 2) + h, kind).wait()


def _stage_b(at, hyt, filt, cfilt, short_w, short_b, skip, dwb):
    bsz, _, seq_len = hyt.shape
    cfg = _tiles(seq_len)
    chans, nb = cfg["ch_b"], cfg["nb"]
    parts = HYENA_ORDER + 2
    long_k = (2 * nb - 1) * TOEP
    conv_k = (2 * (CONV_SPAN // TOEP) - 1) * TOEP
    smem = pl.BlockSpec(memory_space=pltpu.SMEM)
    hbm = pl.BlockSpec(memory_space=pl.ANY)
    chan_major = jax.ShapeDtypeStruct((bsz, D_HYENA, seq_len), F32)
    return pl.pallas_call(
        functools.partial(_stage_b_kernel, bsz=bsz, seq_len=seq_len, cfg=cfg),
        grid=(D_HYENA // chans,),
        in_specs=[smem, smem, smem, smem,
                  pl.BlockSpec((HYENA_ORDER, chans * cfg["long"]["pitch"], LANES), lambda g: (0, g, 0)),
                  pl.BlockSpec((1, chans * cfg["conv"]["pitch"], LANES), lambda g: (0, g, 0)),
                  hbm, hbm],
        out_specs=[hbm, hbm],
        out_shape=[chan_major, chan_major],
        scratch_shapes=[
            pltpu.VMEM((6, parts, bsz, seq_len), F32),
            pltpu.VMEM((4, 2, bsz, seq_len), F32),
            pltpu.VMEM((2, HYENA_ORDER + 1, bsz, seq_len), F32),
            *[pltpu.VMEM((HYENA_ORDER, long_k, TOEP), BF16)] * 4,
            *[pltpu.VMEM((conv_k, TOEP), BF16)] * 4,
            pltpu.VMEM((2, nb * bsz, TOEP), BF16),
            pltpu.VMEM((2, nb * bsz, conv_k), BF16),
            pltpu.VMEM((2, nb * bsz, TOEP), F32),
            pltpu.VMEM((2, nb * bsz, TOEP), F32),
            pltpu.SemaphoreType.DMA((6, parts)),
            pltpu.SemaphoreType.DMA((4, 2)),
        ],
        compiler_params=_cparams(("arbitrary",)),
        name=f"stage_b_{seq_len}",
    )(short_w.reshape(-1), short_b.reshape(-1), skip.reshape(-1), dwb.reshape(-1), filt, cfilt, at, hyt)


def _stage_c_kernel(x_ref, sga_ref, sgb_ref, zbt_ref, cvt_ref, mod_ref, lng_ref, lnb_ref, wpw_ref, whyo_ref,
                    wout_ref, ln1g_ref, ln1b_ref, w1_ref, b1_ref, w2_ref, b2_ref, ln2g_ref, ln2b_ref, o_ref,
                    *, ff_chunk):
    gate1 = mod_ref[0, 2:3, :]
    shift2 = mod_ref[0, 3:4, :]
    scale2 = mod_ref[0, 4:5, :]
    gate2 = mod_ref[0, 5:6, :]
    half = x_ref.shape[1] // 2

    def mixer(rows):
        a2 = _ln_rows(cvt_ref[0, :, rows].T) * lng_ref[...] + lnb_ref[...]
        a2 = (a2 * jax.nn.sigmoid(a2)).astype(BF16)
        ya = jnp.dot(a2, wpw_ref[...], preferred_element_type=F32)
        zb = zbt_ref[0, :, rows].T.astype(BF16)
        yb = jnp.dot(zb, whyo_ref[...], preferred_element_type=F32)
        m = sga_ref[0, rows, :] * ya + sgb_ref[0, rows, :] * yb
        mix = jnp.dot(m.astype(BF16), wout_ref[...], preferred_element_type=F32)
        x1 = _ln_rows(DEEPNORM_ALPHA * x_ref[0, rows, :] + gate1 * mix) * ln1g_ref[...] + ln1b_ref[...]
        return x1, (_ln_rows(x1) * (1.0 + scale2) + shift2).astype(BF16)

    def mlp(x1, h2, rows):
        f = jnp.zeros(x1.shape, F32) + b2_ref[...]
        for j in range(D_FF // ff_chunk):
            cols = slice(j * ff_chunk, (j + 1) * ff_chunk)
            u = jnp.maximum(jnp.dot(h2, w1_ref[:, cols], preferred_element_type=F32) + b1_ref[:, cols], 0.0)
            f = f + jnp.dot((u * u).astype(BF16), w2_ref[cols, :], preferred_element_type=F32)
        o_ref[0, rows, :] = _ln_rows(DEEPNORM_ALPHA * x1 + gate2 * f) * ln2g_ref[...] + ln2b_ref[...]

    rows_a, rows_b = slice(0, half), slice(half, 2 * half)
    x1_a, h2_a = mixer(rows_a)
    x1_b, h2_b = mixer(rows_b)
    mlp(x1_a, h2_a, rows_a)
    mlp(x1_b, h2_b, rows_b)


def _stage_c(x, sga, sgb, zbt, cvt, mod, lng, lnb, wpw, whyo, wout, ln1g, ln1b, w1, b1, w2, b2, ln2g, ln2b):
    bsz, seq_len, _ = x.shape
    tt = _tiles(seq_len)["tt_c"]
    tok = pl.BlockSpec((1, tt, D_MODEL), lambda b, i: (b, i, 0))
    chan = pl.BlockSpec((1, D_HYENA, tt), lambda b, i: (b, 0, i))
    return pl.pallas_call(
        functools.partial(_stage_c_kernel, ff_chunk=D_MODEL),
        grid=(bsz, seq_len // tt),
        in_specs=[tok, tok, tok, chan, chan,
                  pl.BlockSpec((1, N_MOD, D_MODEL), lambda b, i: (b, 0, 0)),
                  _const_spec((1, D_CONV)), _const_spec((1, D_CONV)), _const_spec((D_CONV, D_MODEL)),
                  _const_spec((D_HYENA, D_MODEL)), _const_spec((D_MODEL, D_MODEL)),
                  _const_spec((1, D_MODEL)), _const_spec((1, D_MODEL)),
                  _const_spec((D_MODEL, D_FF)), _const_spec((1, D_FF)),
                  _const_spec((D_FF, D_MODEL)), _const_spec((1, D_MODEL)),
                  _const_spec((1, D_MODEL)), _const_spec((1, D_MODEL))],
        out_specs=tok,
        out_shape=jax.ShapeDtypeStruct((bsz, seq_len, D_MODEL), F32),
        compiler_params=_cparams(("parallel", "parallel")),
        name=f"stage_c_{seq_len}",
    )(x, sga, sgb, zbt, cvt, mod, lng, lnb, wpw, whyo, wout, ln1g, ln1b, w1, b1, w2, b2, ln2g, ln2b)


def _encoder(x, mod, filt, p):
    sga, sgb, at, hyt = _stage_a(x, mod, p["wct"], p["wg"])
    zbt, cvt = _stage_b(at, hyt, filt, p["cfilt"], p["short_w"], p["short_b"], p["skip"], p["dwb"])
    return _stage_c(x, sga, sgb, zbt, cvt, mod, p["lng"], p["lnb"], p["wpw"], p["whyo"], p["wout"],
                    p["ln1g"], p["ln1b"], p["w1"], p["b1"], p["w2"], p["b2"], p["ln2g"], p["ln2b"])


def _prepare(w_in, conv_dw_w, conv_dw_b, conv_ln_g, conv_ln_b, conv_pw_w, hy_short_w, hy_short_b, hy_skip,
             hy_out_w, w_out, ln1_g, ln1_b, mlp_w1, mlp_b1, mlp_w2, mlp_b2, ln2_g, ln2_b, l=0):
    row = lambda v: v.reshape(1, -1)
    s2 = 2 * D_CONV + (HYENA_ORDER + 1) * D_HYENA
    w_in_l = w_in[l]
    return dict(
        wct=w_in_l[:, :s2].T.astype(BF16),
        wg=w_in_l[:, s2:].astype(BF16),
        cfilt=_conv_filter_words(conv_dw_w[l]), dwb=conv_dw_b[l],
        lng=row(conv_ln_g[l]), lnb=row(conv_ln_b[l]), wpw=conv_pw_w[l].astype(BF16),
        short_w=hy_short_w[l], short_b=hy_short_b[l], skip=hy_skip[l],
        whyo=hy_out_w[l].astype(BF16), wout=w_out[l].astype(BF16),
        ln1g=row(ln1_g[l]), ln1b=row(ln1_b[l]),
        w1=mlp_w1[l].astype(BF16), b1=row(mlp_b1[l]), w2=mlp_w2[l].astype(BF16), b2=row(mlp_b2[l]),
        ln2g=row(ln2_g[l]), ln2b=row(ln2_b[l]),
    )


def kernel(x_prompt, x_sample, c_prompt, c_sample, w_ada, b_ada, w_in, conv_dw_w, conv_dw_b, conv_ln_g, conv_ln_b, conv_pw_w, hy_short_w, hy_short_b, hy_ffn_w1, hy_ffn_b1, hy_sin_freq, hy_ffn_w2, hy_ffn_b2, hy_ffn_w3, hy_skip, hy_out_w, w_out, ln1_g, ln1_b, mlp_w1, mlp_b1, mlp_w2, mlp_b2, ln2_g, ln2_b):
    l = 0
    p = _prepare(w_in, conv_dw_w, conv_dw_b, conv_ln_g, conv_ln_b, conv_pw_w, hy_short_w, hy_short_b, hy_skip,
                 hy_out_w, w_out, ln1_g, ln1_b, mlp_w1, mlp_b1, mlp_w2, mlp_b2, ln2_g, ln2_b, l)
    nb_p = x_prompt.shape[0]
    mod = _mod(jnp.concatenate([c_prompt, c_sample], axis=0), w_ada[l], b_ada[l])
    mod = mod.reshape(-1, N_MOD, D_MODEL)
    outs = []
    for x, m in ((x_prompt, mod[:nb_p]), (x_sample, mod[nb_p:])):
        filt = _filters(x.shape[1], hy_ffn_w1[l], hy_ffn_b1[l], hy_sin_freq[l],
                        hy_ffn_w2[l], hy_ffn_b2[l], hy_ffn_w3[l])
        outs.append(_encoder(x, m, filt, p))
    return tuple(outs)
```
